```python
import math
import jax, jax.numpy as jnp
from jax import lax
import numpy as np

D_MODEL = 1024
BATCH = 8
SEQ = 8192
DEPTH = 4

N_MEM = 256
BLOCK_Q = 128
MIX_WIDTH = D_MODEL
MEM_WIDTH = D_MODEL // 4
MAIN_WIDTH = MIX_WIDTH - MEM_WIDTH
MEM_HEADS = 4
MEM_DIM = MEM_WIDTH // MEM_HEADS
MLA_V = 64
MLA_HEADS = MAIN_WIDTH // MLA_V
MLA_NOPE = 64
MLA_ROPE = 32
MLA_QK = MLA_NOPE + MLA_ROPE
Q_LORA = 384
KV_LORA = 256
ROPE_THETA = 10000.0
MAX_POS_OFFSET = 4096
FOX_DIM = 64
FOX_HEADS = MAIN_WIDTH // FOX_DIM
EPS = 1e-6
N_A = (DEPTH + 1) // 2
N_B = DEPTH // 2
A_IN = Q_LORA + KV_LORA + MLA_ROPE + MEM_WIDTH + MIX_WIDTH
B_IN = 3 * MAIN_WIDTH + FOX_HEADS + MEM_WIDTH + MIX_WIDTH

kernel_name = "hybrid_mla_fox_memory_trunk"


def rms_norm(x, g):
    xf = x.astype(jnp.float32)
    y = xf * lax.rsqrt(jnp.mean(xf * xf, axis=-1, keepdims=True) + EPS)
    return (y * g.astype(jnp.float32)).astype(x.dtype)


def split_cols(a, sizes):
    idx = np.cumsum(sizes)[:-1].tolist()
    return jnp.split(a, idx, axis=-1)


def rope(x, pos):
    half = x.shape[-1] // 2
    inv = ROPE_THETA ** (-jnp.arange(half, dtype=jnp.float32) / half)
    ang = pos.astype(jnp.float32)[..., None] * inv
    if x.ndim == 4:
        ang = ang[:, :, None, :]
    cos, sin = jnp.cos(ang), jnp.sin(ang)
    x1 = x[..., :half].astype(jnp.float32)
    x2 = x[..., half:].astype(jnp.float32)
    return jnp.concatenate([x1 * cos - x2 * sin, x2 * cos + x1 * sin], axis=-1).astype(x.dtype)


def causal_block_attention(q, k, v, scale, log_f_cum=None):
    B, S, H, Dk = q.shape
    nb = S // BLOCK_Q
    qb = q.reshape(B, nb, BLOCK_Q, H, Dk).swapaxes(0, 1)
    blk = jnp.arange(nb)
    key_pos = jnp.arange(S)
    if log_f_cum is not None:
        f_t = log_f_cum.transpose(0, 2, 1)
        fb = jnp.moveaxis(f_t.reshape(B, H, nb, BLOCK_Q), 2, 0)
        xs = (blk, qb, fb)
    else:
        xs = (blk, qb)

    def one_block(args):
        i, q_i = args[0], args[1]
        s = jnp.einsum('bqhd,bkhd->bhqk', q_i, k,
                       preferred_element_type=jnp.float32) * scale
        if log_f_cum is not None:
            s = s + (args[2][..., None].astype(jnp.float32) - f_t[:, :, None, :].astype(jnp.float32))
        q_pos = i * BLOCK_Q + jnp.arange(BLOCK_Q)
        mask = key_pos[None, :] <= q_pos[:, None]
        s = jnp.where(mask, s, -jnp.inf)
        p = jax.nn.softmax(s, axis=-1)
        return jnp.einsum('bhqk,bkhd->bqhd', p.astype(v.dtype), v)

    out = lax.map(one_block, xs)
    return out.swapaxes(0, 1).reshape(B, S, H, v.shape[-1])


def memory_attention(q, k, v):
    s = jnp.einsum('bshd,bmhd->bhsm', q, k,
                   preferred_element_type=jnp.float32) * (MEM_DIM ** -0.5)
    p = jax.nn.softmax(s, axis=-1)
    return jnp.einsum('bhsm,bmhd->bshd', p.astype(v.dtype), v)


def mla_mixer(c_q, c_kv, k_r, pos, q_norm, kv_norm, w_q_up, w_kv_up):
    B, S, _ = c_q.shape
    q = (rms_norm(c_q, q_norm) @ w_q_up).reshape(B, S, MLA_HEADS, MLA_QK)
    q_nope, q_rope = q[..., :MLA_NOPE], rope(q[..., MLA_NOPE:], pos)
    kv = (rms_norm(c_kv, kv_norm) @ w_kv_up).reshape(B, S, MLA_HEADS, MLA_NOPE + MLA_V)
    k_nope, v = kv[..., :MLA_NOPE], kv[..., MLA_NOPE:]
    k_rope = jnp.broadcast_to(rope(k_r, pos)[:, :, None, :], (B, S, MLA_HEADS, MLA_ROPE))
    qf = jnp.concatenate([q_nope, q_rope], axis=-1)
    kf = jnp.concatenate([k_nope, k_rope], axis=-1)
    out = causal_block_attention(qf, kf, v, MLA_QK ** -0.5)
    return out.reshape(B, S, MAIN_WIDTH)


def fox_mixer(q, k, v, f_logit, b_f):
    B, S, _ = q.shape
    log_f = jax.nn.log_sigmoid(f_logit.astype(jnp.float32) + b_f.astype(jnp.float32))
    log_f_cum = jnp.cumsum(log_f, axis=1)
    qh = q.reshape(B, S, FOX_HEADS, FOX_DIM)
    kh = k.reshape(B, S, FOX_HEADS, FOX_DIM)
    vh = v.reshape(B, S, FOX_HEADS, FOX_DIM)
    out = causal_block_attention(qh, kh, vh, FOX_DIM ** -0.5, log_f_cum)
    return out.reshape(B, S, MAIN_WIDTH)


def setup_inputs(seed: int = 0) -> dict:
    key = jax.random.key(seed)
    ks = jax.random.split(key, 16)
    f32 = jnp.float32

    def nrm(k, shape, fan_in):
        return jax.random.normal(k, shape, f32) * (fan_in ** -0.5)

    def gain(k, shape):
        return 1.0 + 0.01 * jax.random.normal(k, shape, f32)

    x = jax.random.normal(ks[0], (BATCH, SEQ, D_MODEL), f32)
    mem = jax.random.normal(ks[1], (BATCH, N_MEM, D_MODEL), f32)
    positions = (jax.random.randint(ks[2], (BATCH, 1), 0, MAX_POS_OFFSET, dtype=jnp.int32)
                 + jnp.arange(SEQ, dtype=jnp.int32)[None, :])
    return {
        "x": x,
        "mem": mem,
        "positions": positions,
        "norm_pre": gain(ks[3], (DEPTH, D_MODEL)),
        "norm_post": gain(ks[4], (DEPTH, D_MODEL)),
        "mem_norm": gain(ks[5], (D_MODEL,)),
        "w_mem_kv": nrm(ks[6], (DEPTH, D_MODEL, 2 * MEM_WIDTH), D_MODEL),
        "w_out": nrm(ks[7], (DEPTH, MIX_WIDTH, D_MODEL), MIX_WIDTH),
        "w_in_a": nrm(ks[8], (N_A, D_MODEL, A_IN), D_MODEL),
        "q_norm_a": gain(ks[9], (N_A, Q_LORA)),
        "kv_norm_a": gain(ks[10], (N_A, KV_LORA)),
        "w_q_up_a": nrm(ks[11], (N_A, Q_LORA, MLA_HEADS * MLA_QK), Q_LORA),
        "w_kv_up_a": nrm(ks[12], (N_A, KV_LORA, MLA_HEADS * (MLA_NOPE + MLA_V)), KV_LORA),
        "w_in_b": nrm(ks[13], (N_B, D_MODEL, B_IN), D_MODEL),
        "b_f": jax.random.uniform(ks[14], (N_B, FOX_HEADS), f32, 1.0, 4.0),
    }


def reference(x, mem, positions, norm_pre, norm_post, mem_norm, w_mem_kv, w_out,
              w_in_a, q_norm_a, kv_norm_a, w_q_up_a, w_kv_up_a, w_in_b, b_f):
    B, S, _ = x.shape
    M = mem.shape[1]
    mem_n = rms_norm(mem, mem_norm)
    h = x
    for i in range(DEPTH):
        j = i // 2
        hn = rms_norm(h, norm_pre[i])
        mkv = (mem_n @ w_mem_kv[i]).reshape(M, 2, MEM_HEADS, MEM_DIM) if False else \
            (mem_n @ w_mem_kv[i]).reshape(B, M, 2, MEM_HEADS, MEM_DIM)
        k_m, v_m = mkv[:, :, 0], mkv[:, :, 1]
        if i % 2 == 0:
            proj = hn @ w_in_a[j]
            c_q, c_kv, k_r, q_m, gate = split_cols(
                proj, [Q_LORA, KV_LORA, MLA_ROPE, MEM_WIDTH, MIX_WIDTH])
            main = mla_mixer(c_q, c_kv, k_r, positions, q_norm_a[j], kv_norm_a[j],
                             w_q_up_a[j], w_kv_up_a[j])
        else:
            proj = hn @ w_in_b[j]
            q, k, v, f_logit, q_m, gate = split_cols(
                proj, [MAIN_WIDTH, MAIN_WIDTH, MAIN_WIDTH, FOX_HEADS, MEM_WIDTH, MIX_WIDTH])
            main = fox_mixer(q, k, v, f_logit, b_f[j])
        mem_out = memory_attention(q_m.reshape(B, S, MEM_HEADS, MEM_DIM), k_m, v_m)
        y = jnp.concatenate([main, mem_out.reshape(B, S, MEM_WIDTH)], axis=-1) * jax.nn.silu(gate)
        h = h + rms_norm(y @ w_out[i], norm_post[i])
    return h
```

```python
import functools
import math

import jax
import jax.numpy as jnp
from jax import lax
from jax.experimental import pallas as pl
from jax.experimental.pallas import tpu as pltpu

D_MODEL = 1024
N_MEM = 256
MEM_WIDTH = D_MODEL // 4
MAIN_WIDTH = D_MODEL - MEM_WIDTH
MEM_HEADS = 4
MEM_DIM = MEM_WIDTH // MEM_HEADS
HEADS = 12
HEAD_V = 64
MLA_NOPE = 64
MLA_ROPE = 32
MLA_QK = MLA_NOPE + MLA_ROPE
ROPE_HALF = MLA_ROPE // 2
Q_LORA = 384
KV_LORA = 256
ROPE_THETA = 10000.0
FOX_DIM = 64
EPS = 1e-6
LOG2E = math.log2(math.e)

BF16_ROWS = 16
AUG_ROWS = BF16_ROWS
FOX_QK_ROWS = FOX_DIM + AUG_ROWS
V_ROWS = HEAD_V + BF16_ROWS
F_ROWS = BF16_ROWS
NEG_BIG = -1e30
VMEM_LIMIT_BYTES = 56 * 1024 * 1024

F32 = jnp.float32
BF16 = jnp.bfloat16


def _dot(a, b):
    return jnp.dot(a, b, preferred_element_type=F32)


def _rms_rows(x, g):
    ms = jnp.mean(x * x, axis=0, keepdims=True)
    return x * lax.rsqrt(ms + EPS) * g


def _split3(x):
    hi = x.astype(BF16).astype(F32)
    r = x - hi
    mid = r.astype(BF16).astype(F32)
    lo = (r - mid).astype(BF16).astype(F32)
    return hi, mid, lo


def _rope_rows(x1, x2, cos, sin):
    return x1 * cos - x2 * sin, x2 * cos + x1 * sin


def _const_spec(shape):
    return pl.BlockSpec(shape, lambda *_: (0,) * len(shape))


def _params(semantics):
    return pltpu.CompilerParams(dimension_semantics=semantics,
                                vmem_limit_bytes=VMEM_LIMIT_BYTES)


def _mem_kv_kernel(mem_ref, g_ref, wk_ref, wvT_ref, km_ref, vmT_ref, *, depth):
    x = mem_ref[...]
    ms = jnp.mean(x * x, axis=-1, keepdims=True)
    xn = (x * lax.rsqrt(ms + EPS) * g_ref[...]).astype(BF16)
    for i in range(depth):
        for h in range(MEM_HEADS):
            km_ref[i, h] = _dot(xn, wk_ref[i, h]).astype(BF16)
            vmT_ref[i, h] = lax.dot_general(
                wvT_ref[i, h], xn, (((1,), (1,)), ((), ())),
                preferred_element_type=F32).astype(BF16)


def _mem_kv(mem, mem_norm, w_mem_kv):
    B, M, D = mem.shape
    depth = w_mem_kv.shape[0]
    w = w_mem_kv.reshape(depth, D, 2, MEM_HEADS, MEM_DIM)
    wk = jnp.transpose(w[:, :, 0], (0, 2, 1, 3)).astype(BF16)
    wvT = jnp.transpose(w[:, :, 1], (0, 2, 3, 1)).astype(BF16)
    return pl.pallas_call(
        functools.partial(_mem_kv_kernel, depth=depth),
        grid=(B,),
        in_specs=[
            pl.BlockSpec((None, M, D), lambda b: (b, 0, 0)),
            _const_spec((1, D)),
            _const_spec((depth, MEM_HEADS, D, MEM_DIM)),
            _const_spec((depth, MEM_HEADS, MEM_DIM, D)),
        ],
        out_specs=[
            pl.BlockSpec((None, depth, MEM_HEADS, M, MEM_DIM), lambda b: (b, 0, 0, 0, 0)),
            pl.BlockSpec((None, depth, MEM_HEADS, MEM_DIM, M), lambda b: (b, 0, 0, 0, 0)),
        ],
        out_shape=[
            jax.ShapeDtypeStruct((B, depth, MEM_HEADS, M, MEM_DIM), BF16),
            jax.ShapeDtypeStruct((B, depth, MEM_HEADS, MEM_DIM, M), BF16),
        ],
        compiler_params=_params(("arbitrary",)),
        name="mem_kv",
    )(mem, mem_norm.reshape(1, D), wk, wvT)


def _ones_row_block(cols):
    row = lax.broadcasted_iota(jnp.int32, (BF16_ROWS, cols), 0)
    return jnp.where(row == 0, 1.0, 0.0).astype(BF16)


def _store_blocks(ref, h, rows, val, bk):
    nblk = ref.shape[1]
    for j in range(nblk):
        ref[h, j, rows, :] = val[:, j * bk:(j + 1) * bk]


def _proj_mla_kernel(h_ref, pos_ref, inv_ref, g_ref, w_ref, qn_ref, kvn_ref, wq_ref, wk_ref, wv_ref,
                     qT_ref, kT_ref, vT_ref, qmT_ref, gT_ref, *, bk):
    R = h_ref.shape[-1]
    hn = _rms_rows(h_ref[...], g_ref[...]).astype(BF16)
    o_kv = Q_LORA
    o_kr = o_kv + KV_LORA
    o_qm = o_kr + MLA_ROPE
    o_g = o_qm + MEM_WIDTH

    ang = inv_ref[...] * pos_ref[...].astype(F32)
    cos, sin = jnp.cos(ang), jnp.sin(ang)

    cqn = _rms_rows(_dot(w_ref[0:o_kv, :], hn), qn_ref[...]).astype(BF16)
    q = _dot(wq_ref[...], cqn) * (MLA_QK ** -0.5 * LOG2E)
    for h in range(HEADS):
        qh = q[h * MLA_QK:(h + 1) * MLA_QK]
        r1, r2 = _rope_rows(qh[MLA_NOPE:MLA_NOPE + ROPE_HALF], qh[MLA_NOPE + ROPE_HALF:], cos, sin)
        qT_ref[h, 0:MLA_NOPE, :] = qh[0:MLA_NOPE].astype(BF16)
        qT_ref[h, MLA_NOPE:MLA_NOPE + ROPE_HALF, :] = r1.astype(BF16)
        qT_ref[h, MLA_NOPE + ROPE_HALF:MLA_QK, :] = r2.astype(BF16)

    ckvn = _rms_rows(_dot(w_ref[o_kv:o_kr, :], hn), kvn_ref[...]).astype(BF16)
    kr = _dot(w_ref[o_kr:o_qm, :], hn)
    k1, k2 = _rope_rows(kr[0:ROPE_HALF], kr[ROPE_HALF:], cos, sin)
    krr = jnp.concatenate([k1, k2], axis=0).astype(BF16)
    kn = _dot(wk_ref[...], ckvn).astype(BF16)
    v = _dot(wv_ref[...], ckvn).astype(BF16)
    ones_blk = _ones_row_block(R)
    for h in range(HEADS):
        _store_blocks(kT_ref, h, slice(0, MLA_NOPE), kn[h * MLA_NOPE:(h + 1) * MLA_NOPE], bk)
        _store_blocks(kT_ref, h, slice(MLA_NOPE, MLA_QK), krr, bk)
        _store_blocks(vT_ref, h, slice(0, HEAD_V), v[h * HEAD_V:(h + 1) * HEAD_V], bk)
        _store_blocks(vT_ref, h, slice(HEAD_V, V_ROWS), ones_blk, bk)

    qmT_ref[...] = (_dot(w_ref[o_qm:o_g, :], hn) * (MEM_DIM ** -0.5 * LOG2E)).astype(BF16)
    gate = _dot(w_ref[o_g:, :], hn)
    gT_ref[...] = (gate * jax.nn.sigmoid(gate)).astype(BF16)


def _proj_fox_kernel(h_ref, g_ref, w_ref, bf_ref, tri_ref,
                     qT_ref, kT_ref, vT_ref, qmT_ref, gT_ref, carry_ref, *, bk):
    R = h_ref.shape[-1]

    @pl.when(pl.program_id(1) == 0)
    def _():
        carry_ref[...] = jnp.zeros_like(carry_ref)

    hn = _rms_rows(h_ref[...], g_ref[...]).astype(BF16)
    o_k = MAIN_WIDTH
    o_v = 2 * MAIN_WIDTH
    o_qm = 3 * MAIN_WIDTH
    o_g = o_qm + MEM_WIDTH
    o_f = o_g + D_MODEL

    z = _dot(w_ref[o_f:, :], hn) + bf_ref[...]
    logf = (jnp.minimum(z, 0.0) - jnp.log1p(jnp.exp(-jnp.abs(z)))) * LOG2E
    hi, mid, lo = _split3(logf)
    parts = jnp.concatenate([hi, mid, lo], axis=0).astype(BF16)
    c = _dot(parts, tri_ref[...])
    F = carry_ref[:, 0:1] + (c[0:F_ROWS] + c[F_ROWS:2 * F_ROWS] + c[2 * F_ROWS:])
    carry_ref[...] = jnp.broadcast_to(carry_ref[:, 0:1] + jnp.sum(logf, axis=1, keepdims=True),
                                      carry_ref.shape)
    f_hi, f_mid, f_lo = _split3(F)

    q = (_dot(w_ref[0:o_k, :], hn) * (FOX_DIM ** -0.5 * LOG2E)).astype(BF16)
    k = _dot(w_ref[o_k:o_v, :], hn).astype(BF16)
    v = _dot(w_ref[o_v:o_qm, :], hn).astype(BF16)
    row = lax.broadcasted_iota(jnp.int32, (AUG_ROWS, R), 0)
    ones_blk = _ones_row_block(R)
    for h in range(HEADS):
        a, b, c3 = f_hi[h:h + 1], f_mid[h:h + 1], f_lo[h:h + 1]
        q_aug = jnp.where(row < 3, 1.0,
                          jnp.where(row == 3, a, jnp.where(row == 4, b, jnp.where(row == 5, c3, 0.0))))
        k_aug = jnp.where(row == 0, -a,
                          jnp.where(row == 1, -b, jnp.where(row == 2, -c3, jnp.where(row < 6, 1.0, 0.0))))
        qT_ref[h, 0:FOX_DIM, :] = q[h * FOX_DIM:(h + 1) * FOX_DIM]
        qT_ref[h, FOX_DIM:FOX_QK_ROWS, :] = q_aug.astype(BF16)
        _store_blocks(kT_ref, h, slice(0, FOX_DIM), k[h * FOX_DIM:(h + 1) * FOX_DIM], bk)
        _store_blocks(kT_ref, h, slice(FOX_DIM, FOX_QK_ROWS), k_aug.astype(BF16), bk)
        _store_blocks(vT_ref, h, slice(0, HEAD_V), v[h * HEAD_V:(h + 1) * HEAD_V], bk)
        _store_blocks(vT_ref, h, slice(HEAD_V, V_ROWS), ones_blk, bk)

    qmT_ref[...] = (_dot(w_ref[o_qm:o_g, :], hn) * (MEM_DIM ** -0.5 * LOG2E)).astype(BF16)
    gate = _dot(w_ref[o_g:o_f, :], hn)
    gT_ref[...] = (gate * jax.nn.sigmoid(gate)).astype(BF16)


def _proj_out_specs(B, S, R, bk, qk_rows):
    nb = R // bk
    specs = [
        pl.BlockSpec((None, HEADS, qk_rows, R), lambda b, s: (b, 0, 0, s)),
        pl.BlockSpec((None, HEADS, nb, qk_rows, bk), lambda b, s: (b, 0, s, 0, 0)),
        pl.BlockSpec((None, HEADS, nb, V_ROWS, bk), lambda b, s: (b, 0, s, 0, 0)),
        pl.BlockSpec((None, MEM_WIDTH, R), lambda b, s: (b, 0, s)),
        pl.BlockSpec((None, D_MODEL, R), lambda b, s: (b, 0, s)),
    ]
    shapes = [
        jax.ShapeDtypeStruct((B, HEADS, qk_rows, S), BF16),
        jax.ShapeDtypeStruct((B, HEADS, S // bk, qk_rows, bk), BF16),
        jax.ShapeDtypeStruct((B, HEADS, S // bk, V_ROWS, bk), BF16),
        jax.ShapeDtypeStruct((B, MEM_WIDTH, S), BF16),
        jax.ShapeDtypeStruct((B, D_MODEL, S), BF16),
    ]
    return specs, shapes


def _proj_mla(hT, pos3, inv, g_pre, w_in, q_norm, kv_norm, w_q_up, w_kv_up, *, R, bk):
    B, D, S = hT.shape
    w_inT = w_in.T.astype(BF16)
    wqT = w_q_up.T.astype(BF16)
    wkv = w_kv_up.reshape(KV_LORA, HEADS, MLA_NOPE + HEAD_V)
    wkT = wkv[:, :, :MLA_NOPE].reshape(KV_LORA, HEADS * MLA_NOPE).T.astype(BF16)
    wvT = wkv[:, :, MLA_NOPE:].reshape(KV_LORA, HEADS * HEAD_V).T.astype(BF16)
    out_specs, out_shape = _proj_out_specs(B, S, R, bk, MLA_QK)
    return pl.pallas_call(
        functools.partial(_proj_mla_kernel, bk=bk),
        grid=(B, S // R),
        in_specs=[
            pl.BlockSpec((None, D, R), lambda b, s: (b, 0, s)),
            pl.BlockSpec((None, 1, R), lambda b, s: (b, 0, s)),
            _const_spec((ROPE_HALF, 1)),
            _const_spec((D, 1)),
            _const_spec(w_inT.shape),
            _const_spec((Q_LORA, 1)),
            _const_spec((KV_LORA, 1)),
            _const_spec(wqT.shape),
            _const_spec(wkT.shape),
            _const_spec(wvT.shape),
        ],
        out_specs=out_specs,
        out_shape=out_shape,
        compiler_params=_params(("arbitrary", "arbitrary")),
        name="proj_mla",
    )(hT, pos3, inv, g_pre.reshape(D, 1), w_inT, q_norm.reshape(Q_LORA, 1),
      kv_norm.reshape(KV_LORA, 1), wqT, wkT, wvT)


def _proj_fox(hT, g_pre, w_in, b_f, tri, *, R, bk):
    B, D, S = hT.shape
    n_f = w_in.shape[1] - (3 * MAIN_WIDTH + MEM_WIDTH + D_MODEL)
    o_f = 3 * MAIN_WIDTH
    w_main = jnp.concatenate([w_in[:, :o_f], w_in[:, o_f + n_f:]], axis=1)
    w_f = jnp.pad(w_in[:, o_f:o_f + n_f], ((0, 0), (0, F_ROWS - n_f)))
    w_inT = jnp.concatenate([w_main, w_f], axis=1).T.astype(BF16)
    bf = jnp.pad(b_f, (0, F_ROWS - n_f)).reshape(F_ROWS, 1)
    out_specs, out_shape = _proj_out_specs(B, S, R, bk, FOX_QK_ROWS)
    return pl.pallas_call(
        functools.partial(_proj_fox_kernel, bk=bk),
        grid=(B, S // R),
        in_specs=[
            pl.BlockSpec((None, D, R), lambda b, s: (b, 0, s)),
            _const_spec((D, 1)),
            _const_spec(w_inT.shape),
            _const_spec((F_ROWS, 1)),
            _const_spec((R, R)),
        ],
        out_specs=out_specs,
        out_shape=out_shape,
        scratch_shapes=[pltpu.VMEM((F_ROWS, 128), F32)],
        compiler_params=_params(("arbitrary", "arbitrary")),
        name="proj_fox",
    )(hT, g_pre.reshape(D, 1), w_inT, bf, tri)


def _attn_kernel(qT_ref, kT_ref, vT_ref, o_ref):
    qi = pl.program_id(2)
    qT = qT_ref[...]
    bq = qT.shape[-1]
    bk = kT_ref.shape[-1]

    def step(kb, carry, masked):
        m, acc = carry
        sT = lax.dot_general(kT_ref[kb], qT, (((0,), (0,)), ((), ())),
                             preferred_element_type=F32)
        if masked:
            krow = lax.broadcasted_iota(jnp.int32, sT.shape, 0)
            qcol = lax.broadcasted_iota(jnp.int32, sT.shape, 1)
            sT = jnp.where(krow <= qcol, sT, -jnp.inf)
        m_new = jnp.maximum(m, jnp.max(sT, axis=0, keepdims=True))
        p = jnp.exp2(sT - m_new).astype(BF16)
        alpha = jnp.exp2(m - m_new)
        return m_new, alpha * acc + _dot(vT_ref[kb], p)

    carry = (jnp.full((1, bq), NEG_BIG, F32), jnp.zeros((V_ROWS, bq), F32))
    carry = lax.fori_loop(0, qi, functools.partial(step, masked=False), carry)
    _, acc = step(qi, carry, True)
    o_ref[...] = (acc[0:HEAD_V] / acc[HEAD_V:HEAD_V + 1]).astype(BF16)


def _attention(qT, kT, vT, *, bq):
    B, H, dk, S = qT.shape
    nk, bk = kT.shape[2], kT.shape[4]
    assert bq == bk, "the diagonal mask assumes square tiles"
    return pl.pallas_call(
        _attn_kernel,
        grid=(B, H, S // bq),
        in_specs=[
            pl.BlockSpec((None, None, dk, bq), lambda b, h, i: (b, h, 0, i)),
            pl.BlockSpec((None, None, nk, dk, bk), lambda b, h, i: (b, h, 0, 0, 0)),
            pl.BlockSpec((None, None, nk, V_ROWS, bk), lambda b, h, i: (b, h, 0, 0, 0)),
        ],
        out_specs=pl.BlockSpec((None, HEAD_V, bq), lambda b, h, i: (b, h, i)),
        out_shape=jax.ShapeDtypeStruct((B, H * HEAD_V, S), BF16),
        compiler_params=_params(("arbitrary", "arbitrary", "arbitrary")),
        name="causal_attn",
    )(qT, kT, vT)


def _out_kernel(y_ref, qm_ref, g_ref, h_ref, km_ref, vmT_ref, w_ref, gpost_ref, o_ref, y_scr):
    y_scr[0:MAIN_WIDTH, :] = (y_ref[...].astype(F32) * g_ref[0:MAIN_WIDTH, :].astype(F32)).astype(BF16)
    for hm in range(MEM_HEADS):
        rows = slice(hm * MEM_DIM, (hm + 1) * MEM_DIM)
        sT = _dot(km_ref[hm], qm_ref[rows, :])
        p = jnp.exp2(sT - jnp.max(sT, axis=0, keepdims=True))
        l = jnp.sum(p, axis=0, keepdims=True)
        o = _dot(vmT_ref[hm], p.astype(BF16)) / l
        grow = slice(MAIN_WIDTH + hm * MEM_DIM, MAIN_WIDTH + (hm + 1) * MEM_DIM)
        y_scr[grow, :] = (o * g_ref[grow, :].astype(F32)).astype(BF16)
    out = _dot(w_ref[...], y_scr[...])
    o_ref[...] = h_ref[...] + _rms_rows(out, gpost_ref[...])


def _out_layer(yT, qmT, gT, hT, km, vmT, layer, w_out, g_post, *, R):
    B, D, S = hT.shape
    M = km.shape[3]
    w_outT = w_out.T.astype(BF16)
    return pl.pallas_call(
        _out_kernel,
        grid=(B, S // R),
        in_specs=[
            pl.BlockSpec((None, MAIN_WIDTH, R), lambda b, s: (b, 0, s)),
            pl.BlockSpec((None, MEM_WIDTH, R), lambda b, s: (b, 0, s)),
            pl.BlockSpec((None, D, R), lambda b, s: (b, 0, s)),
            pl.BlockSpec((None, D, R), lambda b, s: (b, 0, s)),
            pl.BlockSpec((None, None, MEM_HEADS, M, MEM_DIM), lambda b, s: (b, layer, 0, 0, 0)),
            pl.BlockSpec((None, None, MEM_HEADS, MEM_DIM, M), lambda b, s: (b, layer, 0, 0, 0)),
            _const_spec((D, D)),
            _const_spec((D, 1)),
        ],
        out_specs=pl.BlockSpec((None, D, R), lambda b, s: (b, 0, s)),
        out_shape=jax.ShapeDtypeStruct((B, D, S), F32),
        scratch_shapes=[pltpu.VMEM((D, R), BF16)],
        input_output_aliases={3: 0},
        compiler_params=_params(("arbitrary", "arbitrary")),
        name="out_layer",
    )(yT, qmT, gT, hT, km, vmT, w_outT, g_post.reshape(D, 1))


def _forward(x, mem, positions, norm_pre, norm_post, mem_norm, w_mem_kv, w_out,
             w_in_a, q_norm_a, kv_norm_a, w_q_up_a, w_kv_up_a, w_in_b, b_f, *, R, bq):
    B, S, D = x.shape
    depth = norm_pre.shape[0]
    bk = bq
    km, vmT = _mem_kv(mem, mem_norm, w_mem_kv)
    hT = jnp.transpose(x, (0, 2, 1))
    pos3 = positions.reshape(B, 1, S)
    inv = (ROPE_THETA ** (-jnp.arange(ROPE_HALF, dtype=F32) / ROPE_HALF)).reshape(ROPE_HALF, 1)
    tri = jnp.triu(jnp.ones((R, R), F32)).astype(BF16)
    for i in range(depth):
        j = i // 2
        if i % 2 == 0:
            qT, kT, vT, qmT, gT = _proj_mla(hT, pos3, inv, norm_pre[i], w_in_a[j], q_norm_a[j],
                                            kv_norm_a[j], w_q_up_a[j], w_kv_up_a[j], R=R, bk=bk)
        else:
            qT, kT, vT, qmT, gT = _proj_fox(hT, norm_pre[i], w_in_b[j], b_f[j], tri, R=R, bk=bk)
        yT = _attention(qT, kT, vT, bq=bq)
        hT = _out_layer(yT, qmT, gT, hT, km, vmT, i, w_out[i], norm_post[i], R=R)
    return jnp.transpose(hT, (0, 2, 1))


def kernel(x, mem, positions, norm_pre, norm_post, mem_norm, w_mem_kv, w_out, w_in_a, q_norm_a,
           kv_norm_a, w_q_up_a, w_kv_up_a, w_in_b, b_f):
    return _forward(x, mem, positions, norm_pre, norm_post, mem_norm, w_mem_kv, w_out, w_in_a,
                    q_norm_a, kv_norm_a, w_q_up_a, w_kv_up_a, w_in_b, b_f, R=512, bq=512)
```

```python
import functools
import math

import jax
import jax.numpy as jnp
from jax import lax
from jax.experimental import pallas as pl
from jax.experimental.pallas import tpu as pltpu

D_MODEL = 1024
N_MEM = 256
MEM_WIDTH = D_MODEL // 4
MAIN_WIDTH = D_MODEL - MEM_WIDTH
MEM_HEADS = 4
MEM_DIM = MEM_WIDTH // MEM_HEADS
HEADS = 12
HEAD_V = 64
MLA_NOPE = 64
MLA_ROPE = 32
MLA_QK = MLA_NOPE + MLA_ROPE
ROPE_HALF = MLA_ROPE // 2
Q_LORA = 384
KV_LORA = 256
ROPE_THETA = 10000.0
FOX_DIM = 64
EPS = 1e-6
LOG2E = math.log2(math.e)

BF16_ROWS = 16
AUG_ROWS = BF16_ROWS
FOX_QK_ROWS = FOX_DIM + AUG_ROWS
V_ROWS = HEAD_V + BF16_ROWS
QK_PAD = 128
STAT_ROWS = 8
F_ROWS = BF16_ROWS
NEG_BIG = -1e30
VMEM_LIMIT_BYTES = 56 * 1024 * 1024

F32 = jnp.float32
BF16 = jnp.bfloat16


def _dot(a, b):
    return jnp.dot(a, b, preferred_element_type=F32)


def _rms_rows(x, g):
    ms = jnp.mean(x * x, axis=0, keepdims=True)
    return x * lax.rsqrt(ms + EPS) * g


def _split3(x):
    hi = x.astype(BF16).astype(F32)
    r = x - hi
    mid = r.astype(BF16).astype(F32)
    lo = (r - mid).astype(BF16).astype(F32)
    return hi, mid, lo


def _rope_rows(x1, x2, cos, sin):
    return x1 * cos - x2 * sin, x2 * cos + x1 * sin


def _const_spec(shape):
    return pl.BlockSpec(shape, lambda *_: (0,) * len(shape))


def _params(semantics):
    return pltpu.CompilerParams(dimension_semantics=semantics,
                                vmem_limit_bytes=VMEM_LIMIT_BYTES)


def _mem_kv_kernel(mem_ref, g_ref, wk_ref, wvT_ref, km_ref, vmT_ref, *, depth):
    x = mem_ref[...]
    ms = jnp.mean(x * x, axis=-1, keepdims=True)
    xn = (x * lax.rsqrt(ms + EPS) * g_ref[...]).astype(BF16)
    for i in range(depth):
        for h in range(MEM_HEADS):
            km_ref[i, h] = _dot(xn, wk_ref[i, h]).astype(BF16)
            vmT_ref[i, h] = lax.dot_general(
                wvT_ref[i, h], xn, (((1,), (1,)), ((), ())),
                preferred_element_type=F32).astype(BF16)


def _mem_kv(mem, mem_norm, w_mem_kv):
    B, M, D = mem.shape
    depth = w_mem_kv.shape[0]
    w = w_mem_kv.reshape(depth, D, 2, MEM_HEADS, MEM_DIM)
    wk = jnp.transpose(w[:, :, 0], (0, 2, 1, 3)).astype(BF16)
    wvT = jnp.transpose(w[:, :, 1], (0, 2, 3, 1)).astype(BF16)
    return pl.pallas_call(
        functools.partial(_mem_kv_kernel, depth=depth),
        grid=(B,),
        in_specs=[
            pl.BlockSpec((None, M, D), lambda b: (b, 0, 0)),
            _const_spec((1, D)),
            _const_spec((depth, MEM_HEADS, D, MEM_DIM)),
            _const_spec((depth, MEM_HEADS, MEM_DIM, D)),
        ],
        out_specs=[
            pl.BlockSpec((None, depth, MEM_HEADS, M, MEM_DIM), lambda b: (b, 0, 0, 0, 0)),
            pl.BlockSpec((None, depth, MEM_HEADS, MEM_DIM, M), lambda b: (b, 0, 0, 0, 0)),
        ],
        out_shape=[
            jax.ShapeDtypeStruct((B, depth, MEM_HEADS, M, MEM_DIM), BF16),
            jax.ShapeDtypeStruct((B, depth, MEM_HEADS, MEM_DIM, M), BF16),
        ],
        compiler_params=_params(("arbitrary",)),
        name="mem_kv",
    )(mem, mem_norm.reshape(1, D), wk, wvT)


def _ones_row_block(cols):
    row = lax.broadcasted_iota(jnp.int32, (BF16_ROWS, cols), 0)
    return jnp.where(row == 0, 1.0, 0.0)


def _store_blocks(ref, h, rows, val, bk):
    nblk = ref.shape[1]
    same = val.shape[1] == bk
    for j in range(nblk):
        ref[h, j, rows, :] = (val if same else val[:, j * bk:(j + 1) * bk]).astype(BF16)


def _store_keys(k_ref, h, parts, bk):
    R = parts[0].shape[1]
    rows = sum(p.shape[0] for p in parts)
    kT = jnp.concatenate(list(parts) + [jnp.zeros((QK_PAD - rows, R), F32)], axis=0)
    for j in range(k_ref.shape[1]):
        k_ref[h, j] = kT[:, j * bk:(j + 1) * bk].T.astype(BF16)


def _proj_mla_kernel(h_ref, pos_ref, inv_ref, g_ref, w_ref, qn_ref, kvn_ref, wq_ref, wk_ref, wv_ref,
                     qT_ref, k_ref, vT_ref, qmT_ref, gT_ref, *, bk):
    R = h_ref.shape[-1]
    hn = _rms_rows(h_ref[...], g_ref[...]).astype(BF16)
    o_kv = Q_LORA
    o_kr = o_kv + KV_LORA
    o_qm = o_kr + MLA_ROPE
    o_g = o_qm + MEM_WIDTH

    ang = inv_ref[...] * pos_ref[...].astype(F32)
    cos, sin = jnp.cos(ang), jnp.sin(ang)

    cqn = _rms_rows(_dot(w_ref[0:o_kv, :], hn), qn_ref[...]).astype(BF16)
    q = _dot(wq_ref[...], cqn) * (MLA_QK ** -0.5 * LOG2E)
    for h in range(HEADS):
        qh = q[h * MLA_QK:(h + 1) * MLA_QK]
        r1, r2 = _rope_rows(qh[MLA_NOPE:MLA_NOPE + ROPE_HALF], qh[MLA_NOPE + ROPE_HALF:], cos, sin)
        qT_ref[h, 0:MLA_NOPE, :] = qh[0:MLA_NOPE].astype(BF16)
        qT_ref[h, MLA_NOPE:MLA_NOPE + ROPE_HALF, :] = r1.astype(BF16)
        qT_ref[h, MLA_NOPE + ROPE_HALF:MLA_QK, :] = r2.astype(BF16)
        qT_ref[h, MLA_QK:, :] = jnp.zeros((QK_PAD - MLA_QK, R), BF16)

    ckvn = _rms_rows(_dot(w_ref[o_kv:o_kr, :], hn), kvn_ref[...]).astype(BF16)
    kr = _dot(w_ref[o_kr:o_qm, :], hn)
    k1, k2 = _rope_rows(kr[0:ROPE_HALF], kr[ROPE_HALF:], cos, sin)
    krr = jnp.concatenate([k1, k2], axis=0)
    kn = _dot(wk_ref[...], ckvn)
    v = _dot(wv_ref[...], ckvn)
    ones_blk = _ones_row_block(bk)
    for h in range(HEADS):
        _store_keys(k_ref, h, [kn[h * MLA_NOPE:(h + 1) * MLA_NOPE], krr], bk)
        _store_blocks(vT_ref, h, slice(0, HEAD_V), v[h * HEAD_V:(h + 1) * HEAD_V], bk)
        _store_blocks(vT_ref, h, slice(HEAD_V, V_ROWS), ones_blk, bk)

    qmT_ref[...] = (_dot(w_ref[o_qm:o_g, :], hn) * (MEM_DIM ** -0.5 * LOG2E)).astype(BF16)
    gate = _dot(w_ref[o_g:, :], hn)
    gT_ref[...] = (gate * jax.nn.sigmoid(gate)).astype(BF16)


def _proj_fox_kernel(h_ref, g_ref, w_ref, bf_ref, tri_ref,
                     qT_ref, k_ref, vT_ref, qmT_ref, gT_ref, carry_ref, *, bk):
    R = h_ref.shape[-1]

    @pl.when(pl.program_id(1) == 0)
    def _():
        carry_ref[...] = jnp.zeros_like(carry_ref)

    hn = _rms_rows(h_ref[...], g_ref[...]).astype(BF16)
    o_k = MAIN_WIDTH
    o_v = 2 * MAIN_WIDTH
    o_qm = 3 * MAIN_WIDTH
    o_g = o_qm + MEM_WIDTH
    o_f = o_g + D_MODEL

    z = _dot(w_ref[o_f:, :], hn) + bf_ref[...]
    logf = (jnp.minimum(z, 0.0) - jnp.log1p(jnp.exp(-jnp.abs(z)))) * LOG2E
    hi, mid, lo = _split3(logf)
    parts = jnp.concatenate([hi, mid, lo], axis=0).astype(BF16)
    c = _dot(parts, tri_ref[...])
    F = carry_ref[:, 0:1] + (c[0:F_ROWS] + c[F_ROWS:2 * F_ROWS] + c[2 * F_ROWS:])
    carry_ref[...] = jnp.broadcast_to(carry_ref[:, 0:1] + jnp.sum(logf, axis=1, keepdims=True),
                                      carry_ref.shape)
    f_hi, f_mid, f_lo = _split3(F)

    q = (_dot(w_ref[0:o_k, :], hn) * (FOX_DIM ** -0.5 * LOG2E)).astype(BF16)
    k = _dot(w_ref[o_k:o_v, :], hn)
    v = _dot(w_ref[o_v:o_qm, :], hn)
    row = lax.broadcasted_iota(jnp.int32, (AUG_ROWS, R), 0)
    ones_blk = _ones_row_block(bk)
    for h in range(HEADS):
        a, b, c3 = f_hi[h:h + 1], f_mid[h:h + 1], f_lo[h:h + 1]
        q_aug = jnp.where(row < 3, 1.0,
                          jnp.where(row == 3, a, jnp.where(row == 4, b, jnp.where(row == 5, c3, 0.0))))
        k_aug = jnp.where(row == 0, -a,
                          jnp.where(row == 1, -b, jnp.where(row == 2, -c3, jnp.where(row < 6, 1.0, 0.0))))
        qT_ref[h, 0:FOX_DIM, :] = q[h * FOX_DIM:(h + 1) * FOX_DIM]
        qT_ref[h, FOX_DIM:FOX_QK_ROWS, :] = q_aug.astype(BF16)
        qT_ref[h, FOX_QK_ROWS:, :] = jnp.zeros((QK_PAD - FOX_QK_ROWS, R), BF16)
        _store_keys(k_ref, h, [k[h * FOX_DIM:(h + 1) * FOX_DIM], k_aug], bk)
        _store_blocks(vT_ref, h, slice(0, HEAD_V), v[h * HEAD_V:(h + 1) * HEAD_V], bk)
        _store_blocks(vT_ref, h, slice(HEAD_V, V_ROWS), ones_blk, bk)

    qmT_ref[...] = (_dot(w_ref[o_qm:o_g, :], hn) * (MEM_DIM ** -0.5 * LOG2E)).astype(BF16)
    gate = _dot(w_ref[o_g:o_f, :], hn)
    gT_ref[...] = (gate * jax.nn.sigmoid(gate)).astype(BF16)


def _proj_out_specs(B, S, R, bk):
    nb = R // bk
    specs = [
        pl.BlockSpec((None, HEADS, QK_PAD, R), lambda b, s: (b, 0, 0, s)),
        pl.BlockSpec((None, HEADS, nb, bk, QK_PAD), lambda b, s: (b, 0, s, 0, 0)),
        pl.BlockSpec((None, HEADS, nb, V_ROWS, bk), lambda b, s: (b, 0, s, 0, 0)),
        pl.BlockSpec((None, MEM_WIDTH, R), lambda b, s: (b, 0, s)),
        pl.BlockSpec((None, D_MODEL, R), lambda b, s: (b, 0, s)),
    ]
    shapes = [
        jax.ShapeDtypeStruct((B, HEADS, QK_PAD, S), BF16),
        jax.ShapeDtypeStruct((B, HEADS, S // bk, bk, QK_PAD), BF16),
        jax.ShapeDtypeStruct((B, HEADS, S // bk, V_ROWS, bk), BF16),
        jax.ShapeDtypeStruct((B, MEM_WIDTH, S), BF16),
        jax.ShapeDtypeStruct((B, D_MODEL, S), BF16),
    ]
    return specs, shapes


def _proj_mla(hT, pos3, inv, g_pre, w_in, q_norm, kv_norm, w_q_up, w_kv_up, *, R, bk):
    B, D, S = hT.shape
    w_inT = w_in.T.astype(BF16)
    wqT = w_q_up.T.astype(BF16)
    wkv = w_kv_up.reshape(KV_LORA, HEADS, MLA_NOPE + HEAD_V)
    wkT = wkv[:, :, :MLA_NOPE].reshape(KV_LORA, HEADS * MLA_NOPE).T.astype(BF16)
    wvT = wkv[:, :, MLA_NOPE:].reshape(KV_LORA, HEADS * HEAD_V).T.astype(BF16)
    out_specs, out_shape = _proj_out_specs(B, S, R, bk)
    return pl.pallas_call(
        functools.partial(_proj_mla_kernel, bk=bk),
        grid=(B, S // R),
        in_specs=[
            pl.BlockSpec((None, D, R), lambda b, s: (b, 0, s)),
            pl.BlockSpec((None, 1, R), lambda b, s: (b, 0, s)),
            _const_spec((ROPE_HALF, 1)),
            _const_spec((D, 1)),
            _const_spec(w_inT.shape),
            _const_spec((Q_LORA, 1)),
            _const_spec((KV_LORA, 1)),
            _const_spec(wqT.shape),
            _const_spec(wkT.shape),
            _const_spec(wvT.shape),
        ],
        out_specs=out_specs,
        out_shape=out_shape,
        compiler_params=_params(("arbitrary", "arbitrary")),
        name="proj_mla",
    )(hT, pos3, inv, g_pre.reshape(D, 1), w_inT, q_norm.reshape(Q_LORA, 1),
      kv_norm.reshape(KV_LORA, 1), wqT, wkT, wvT)


def _proj_fox(hT, g_pre, w_in, b_f, tri, *, R, bk):
    B, D, S = hT.shape
    n_f = w_in.shape[1] - (3 * MAIN_WIDTH + MEM_WIDTH + D_MODEL)
    o_f = 3 * MAIN_WIDTH
    w_main = jnp.concatenate([w_in[:, :o_f], w_in[:, o_f + n_f:]], axis=1)
    w_f = jnp.pad(w_in[:, o_f:o_f + n_f], ((0, 0), (0, F_ROWS - n_f)))
    w_inT = jnp.concatenate([w_main, w_f], axis=1).T.astype(BF16)
    bf = jnp.pad(b_f, (0, F_ROWS - n_f)).reshape(F_ROWS, 1)
    out_specs, out_shape = _proj_out_specs(B, S, R, bk)
    return pl.pallas_call(
        functools.partial(_proj_fox_kernel, bk=bk),
        grid=(B, S // R),
        in_specs=[
            pl.BlockSpec((None, D, R), lambda b, s: (b, 0, s)),
            _const_spec((D, 1)),
            _const_spec(w_inT.shape),
            _const_spec((F_ROWS, 1)),
            _const_spec((R, R)),
        ],
        out_specs=out_specs,
        out_shape=out_shape,
        scratch_shapes=[pltpu.VMEM((F_ROWS, 128), F32)],
        compiler_params=_params(("arbitrary", "arbitrary")),
        name="proj_fox",
    )(hT, g_pre.reshape(D, 1), w_inT, bf, tri)


def _attn_kernel(qT_ref, k_ref, vT_ref, bias_ref, o_ref, s_a, s_b, p_a, p_b, tm_a, tm_b, al_a, al_b, acc_scr, m_scr):
    qi = pl.program_id(2)
    G, _, bq = qT_ref.shape
    bk = s_a.shape[1]
    r = bq // bk
    s_bufs, p_bufs, tm_bufs, al_bufs = (s_a, s_b), (p_a, p_b), (tm_a, tm_b), (al_a, al_b)
    qTs = [qT_ref[g] for g in range(G)]

    def stat(row):
        return jnp.broadcast_to(row, (STAT_ROWS, bq))

    def qk(g, kb, nxt):
        sT = _dot(k_ref[g, kb], qTs[g])
        s_bufs[nxt][g] = sT
        tm_bufs[nxt][g] = stat(jnp.max(sT, axis=0, keepdims=True))

    def pv(g, kb, prev):
        acc_scr[g] = al_bufs[prev][g][0:1] * acc_scr[g] + _dot(vT_ref[g, kb], p_bufs[prev][g])

    def softmax(g, cur, diag):
        sT = s_bufs[cur][g]
        if diag is None:
            tm = tm_bufs[cur][g]
        else:
            sT = sT + bias_ref[diag]
            tm = stat(jnp.max(sT, axis=0, keepdims=True))
        m_old = m_scr[g]
        m_new = jnp.maximum(m_old, tm)
        p_bufs[cur][g] = jnp.exp2(sT - m_new[0:1]).astype(BF16)
        al_bufs[cur][g] = jnp.exp2(m_old - m_new)
        m_scr[g] = m_new

    def stage(j, cur, diag=None, last=False):
        for g in range(G):
            softmax(g, cur, diag)
        if not last:
            for g in range(G):
                qk(g, j + 1, 1 - cur)
        for g in range(G):
            pv(g, jnp.maximum(j - 1, 0), 1 - cur)

    p_b[...] = jnp.zeros_like(p_b)
    acc_scr[...] = jnp.zeros_like(acc_scr)
    al_b[...] = jnp.ones_like(al_b)
    m_scr[...] = jnp.full_like(m_scr, NEG_BIG)
    for g in range(G):
        qk(g, 0, 0)

    def group(jj, c):
        for t in range(r):
            stage(jj * r + t, t % 2)
        return c

    lax.fori_loop(0, qi, group, 0)
    for t in range(r):
        stage(qi * r + t, t % 2, diag=t, last=(t == r - 1))
    for g in range(G):
        pv(g, qi * r + r - 1, (r - 1) % 2)
        acc = acc_scr[g]
        o_ref[g * HEAD_V:(g + 1) * HEAD_V, :] = (acc[0:HEAD_V] / acc[HEAD_V:HEAD_V + 1]).astype(BF16)


def _attention(qT, k, vT, *, bq, G):
    B, H, dk, S = qT.shape
    nk, bk = k.shape[2], k.shape[3]
    r = bq // bk
    assert bq % (2 * bk) == 0, "a query block must span an even number of key tiles"
    krow = jnp.arange(bk, dtype=jnp.int32)[None, :, None] + bk * jnp.arange(r, dtype=jnp.int32)[:, None, None]
    bias = jnp.where(krow <= jnp.arange(bq, dtype=jnp.int32)[None, None, :], 0.0, NEG_BIG).astype(F32)
    return pl.pallas_call(
        _attn_kernel,
        grid=(B, H // G, S // bq),
        in_specs=[
            pl.BlockSpec((None, G, dk, bq), lambda b, h, i: (b, h, 0, i)),
            pl.BlockSpec((None, G, nk, bk, dk), lambda b, h, i: (b, h, 0, 0, 0)),
            pl.BlockSpec((None, G, nk, V_ROWS, bk), lambda b, h, i: (b, h, 0, 0, 0)),
            _const_spec((r, bk, bq)),
        ],
        out_specs=pl.BlockSpec((None, G * HEAD_V, bq), lambda b, h, i: (b, h, i)),
        out_shape=jax.ShapeDtypeStruct((B, H * HEAD_V, S), BF16),
        scratch_shapes=[pltpu.VMEM((G, bk, bq), F32), pltpu.VMEM((G, bk, bq), F32),
                        pltpu.VMEM((G, bk, bq), BF16), pltpu.VMEM((G, bk, bq), BF16),
                        pltpu.VMEM((G, STAT_ROWS, bq), F32), pltpu.VMEM((G, STAT_ROWS, bq), F32),
                        pltpu.VMEM((G, STAT_ROWS, bq), F32), pltpu.VMEM((G, STAT_ROWS, bq), F32),
                        pltpu.VMEM((G, V_ROWS, bq), F32), pltpu.VMEM((G, STAT_ROWS, bq), F32)],
        compiler_params=_params(("arbitrary", "arbitrary", "arbitrary")),
        name="causal_attn",
    )(qT, k, vT, bias)


def _out_kernel(y_ref, qm_ref, g_ref, h_ref, km_ref, vmT_ref, w_ref, gpost_ref, o_ref, y_scr):
    y_scr[0:MAIN_WIDTH, :] = (y_ref[...].astype(F32) * g_ref[0:MAIN_WIDTH, :].astype(F32)).astype(BF16)
    for hm in range(MEM_HEADS):
        rows = slice(hm * MEM_DIM, (hm + 1) * MEM_DIM)
        sT = _dot(km_ref[hm], qm_ref[rows, :])
        p = jnp.exp2(sT - jnp.max(sT, axis=0, keepdims=True))
        l = jnp.sum(p, axis=0, keepdims=True)
        o = _dot(vmT_ref[hm], p.astype(BF16)) / l
        grow = slice(MAIN_WIDTH + hm * MEM_DIM, MAIN_WIDTH + (hm + 1) * MEM_DIM)
        y_scr[grow, :] = (o * g_ref[grow, :].astype(F32)).astype(BF16)
    out = _dot(w_ref[...], y_scr[...])
    o_ref[...] = h_ref[...] + _rms_rows(out, gpost_ref[...])


def _out_layer(yT, qmT, gT, hT, km, vmT, layer, w_out, g_post, *, R):
    B, D, S = hT.shape
    M = km.shape[3]
    w_outT = w_out.T.astype(BF16)
    return pl.pallas_call(
        _out_kernel,
        grid=(B, S // R),
        in_specs=[
            pl.BlockSpec((None, MAIN_WIDTH, R), lambda b, s: (b, 0, s)),
            pl.BlockSpec((None, MEM_WIDTH, R), lambda b, s: (b, 0, s)),
            pl.BlockSpec((None, D, R), lambda b, s: (b, 0, s)),
            pl.BlockSpec((None, D, R), lambda b, s: (b, 0, s)),
            pl.BlockSpec((None, None, MEM_HEADS, M, MEM_DIM), lambda b, s: (b, layer, 0, 0, 0)),
            pl.BlockSpec((None, None, MEM_HEADS, MEM_DIM, M), lambda b, s: (b, layer, 0, 0, 0)),
            _const_spec((D, D)),
            _const_spec((D, 1)),
        ],
        out_specs=pl.BlockSpec((None, D, R), lambda b, s: (b, 0, s)),
        out_shape=jax.ShapeDtypeStruct((B, D, S), F32),
        scratch_shapes=[pltpu.VMEM((D, R), BF16)],
        input_output_aliases={3: 0},
        compiler_params=_params(("arbitrary", "arbitrary")),
        name="out_layer",
    )(yT, qmT, gT, hT, km, vmT, w_outT, g_post.reshape(D, 1))


def _forward(x, mem, positions, norm_pre, norm_post, mem_norm, w_mem_kv, w_out,
             w_in_a, q_norm_a, kv_norm_a, w_q_up_a, w_kv_up_a, w_in_b, b_f, *, R, bq, bk, G):
    B, S, D = x.shape
    depth = norm_pre.shape[0]
    km, vmT = _mem_kv(mem, mem_norm, w_mem_kv)
    hT = jnp.transpose(x, (0, 2, 1))
    pos3 = positions.reshape(B, 1, S)
    inv = (ROPE_THETA ** (-jnp.arange(ROPE_HALF, dtype=F32) / ROPE_HALF)).reshape(ROPE_HALF, 1)
    tri = jnp.triu(jnp.ones((R, R), F32)).astype(BF16)
    for i in range(depth):
        j = i // 2
        if i % 2 == 0:
            qT, kT, vT, qmT, gT = _proj_mla(hT, pos3, inv, norm_pre[i], w_in_a[j], q_norm_a[j],
                                            kv_norm_a[j], w_q_up_a[j], w_kv_up_a[j], R=R, bk=bk)
        else:
            qT, kT, vT, qmT, gT = _proj_fox(hT, norm_pre[i], w_in_b[j], b_f[j], tri, R=R, bk=bk)
        yT = _attention(qT, kT, vT, bq=bq, G=G)
        hT = _out_layer(yT, qmT, gT, hT, km, vmT, i, w_out[i], norm_post[i], R=R)
    return jnp.transpose(hT, (0, 2, 1))


def kernel(x, mem, positions, norm_pre, norm_post, mem_norm, w_mem_kv, w_out, w_in_a, q_norm_a,
           kv_norm_a, w_q_up_a, w_kv_up_a, w_in_b, b_f):
    return _forward(x, mem, positions, norm_pre, norm_post, mem_norm, w_mem_kv, w_out, w_in_a,
                    q_norm_a, kv_norm_a, w_q_up_a, w_kv_up_a, w_in_b, b_f, R=512, bq=512, bk=256, G=4)
```

```python
import functools
import math

import jax
import jax.numpy as jnp
from jax import lax
from jax.experimental import pallas as pl
from jax.experimental.pallas import tpu as pltpu

D_MODEL = 1024
N_MEM = 256
MEM_WIDTH = D_MODEL // 4
MAIN_WIDTH = D_MODEL - MEM_WIDTH
MEM_HEADS = 4
MEM_DIM = MEM_WIDTH // MEM_HEADS
HEADS = 12
HEAD_V = 64
MLA_NOPE = 64
MLA_ROPE = 32
MLA_QK = MLA_NOPE + MLA_ROPE
ROPE_HALF = MLA_ROPE // 2
Q_LORA = 384
KV_LORA = 256
ROPE_THETA = 10000.0
FOX_DIM = 64
EPS = 1e-6
LOG2E = math.log2(math.e)

BF16_ROWS = 16
AUG_ROWS = BF16_ROWS
FOX_QK_ROWS = FOX_DIM + AUG_ROWS
V_ROWS = HEAD_V + BF16_ROWS
QK_PAD = 128
STAT_ROWS = 8
F_ROWS = BF16_ROWS
NEG_BIG = -1e30
VMEM_LIMIT_BYTES = 56 * 1024 * 1024

F32 = jnp.float32
BF16 = jnp.bfloat16


def _dot(a, b):
    return jnp.dot(a, b, preferred_element_type=F32)


def _rms_rows(x, g):
    ms = jnp.mean(x * x, axis=0, keepdims=True)
    return x * lax.rsqrt(ms + EPS) * g


def _split3(x):
    hi = x.astype(BF16).astype(F32)
    r = x - hi
    mid = r.astype(BF16).astype(F32)
    lo = (r - mid).astype(BF16).astype(F32)
    return hi, mid, lo


def _rope_rows(x1, x2, cos, sin):
    return x1 * cos - x2 * sin, x2 * cos + x1 * sin


def _const_spec(shape):
    return pl.BlockSpec(shape, lambda *_: (0,) * len(shape))


def _params(semantics):
    return pltpu.CompilerParams(dimension_semantics=semantics,
                                vmem_limit_bytes=VMEM_LIMIT_BYTES)


def _mem_kv_kernel(mem_ref, g_ref, wk_ref, wvT_ref, km_ref, vmT_ref, *, depth):
    x = mem_ref[...]
    ms = jnp.mean(x * x, axis=-1, keepdims=True)
    xn = (x * lax.rsqrt(ms + EPS) * g_ref[...]).astype(BF16)
    for i in range(depth):
        for h in range(MEM_HEADS):
            km_ref[i, h] = _dot(xn, wk_ref[i, h]).astype(BF16)
            vmT_ref[i, h] = lax.dot_general(
                wvT_ref[i, h], xn, (((1,), (1,)), ((), ())),
                preferred_element_type=F32).astype(BF16)


def _mem_kv(mem, mem_norm, w_mem_kv):
    B, M, D = mem.shape
    depth = w_mem_kv.shape[0]
    w = w_mem_kv.reshape(depth, D, 2, MEM_HEADS, MEM_DIM)
    wk = jnp.transpose(w[:, :, 0], (0, 2, 1, 3)).astype(BF16)
    wvT = jnp.transpose(w[:, :, 1], (0, 2, 3, 1)).astype(BF16)
    return pl.pallas_call(
        functools.partial(_mem_kv_kernel, depth=depth),
        grid=(B,),
        in_specs=[
            pl.BlockSpec((None, M, D), lambda b: (b, 0, 0)),
            _const_spec((1, D)),
            _const_spec((depth, MEM_HEADS, D, MEM_DIM)),
            _const_spec((depth, MEM_HEADS, MEM_DIM, D)),
        ],
        out_specs=[
            pl.BlockSpec((None, depth, MEM_HEADS, M, MEM_DIM), lambda b: (b, 0, 0, 0, 0)),
            pl.BlockSpec((None, depth, MEM_HEADS, MEM_DIM, M), lambda b: (b, 0, 0, 0, 0)),
        ],
        out_shape=[
            jax.ShapeDtypeStruct((B, depth, MEM_HEADS, M, MEM_DIM), BF16),
            jax.ShapeDtypeStruct((B, depth, MEM_HEADS, MEM_DIM, M), BF16),
        ],
        compiler_params=_params(("arbitrary",)),
        name="mem_kv",
    )(mem, mem_norm.reshape(1, D), wk, wvT)


def _ones_row_block(cols):
    row = lax.broadcasted_iota(jnp.int32, (BF16_ROWS, cols), 0)
    return jnp.where(row == 0, 1.0, 0.0)


def _store_blocks(ref, h, rows, val, bk):
    nblk = ref.shape[1]
    same = val.shape[1] == bk
    for j in range(nblk):
        ref[h, j, rows, :] = (val if same else val[:, j * bk:(j + 1) * bk]).astype(BF16)


def _store_keys(k_ref, h, parts, bk):
    R = parts[0].shape[1]
    rows = sum(p.shape[0] for p in parts)
    kT = jnp.concatenate(list(parts) + [jnp.zeros((QK_PAD - rows, R), F32)], axis=0)
    for j in range(k_ref.shape[1]):
        k_ref[h, j] = kT[:, j * bk:(j + 1) * bk].T.astype(BF16)


def _proj_mla_kernel(h_ref, pos_ref, inv_ref, g_ref, w_ref, qn_ref, kvn_ref, wq_ref, wk_ref, wv_ref,
                     qT_ref, k_ref, vT_ref, qmT_ref, gT_ref, *hT_out, bk):
    if hT_out:
        hT = h_ref[...].T
        hT_out[0][...] = hT
    else:
        hT = h_ref[...]
    R = hT.shape[-1]
    hn = _rms_rows(hT, g_ref[...]).astype(BF16)
    o_kv = Q_LORA
    o_kr = o_kv + KV_LORA
    o_qm = o_kr + MLA_ROPE
    o_g = o_qm + MEM_WIDTH

    ang = inv_ref[...] * pos_ref[...].astype(F32)
    cos, sin = jnp.cos(ang), jnp.sin(ang)

    cqn = _rms_rows(_dot(w_ref[0:o_kv, :], hn), qn_ref[...]).astype(BF16)
    q = _dot(wq_ref[...], cqn) * (MLA_QK ** -0.5 * LOG2E)
    for h in range(HEADS):
        qh = q[h * MLA_QK:(h + 1) * MLA_QK]
        r1, r2 = _rope_rows(qh[MLA_NOPE:MLA_NOPE + ROPE_HALF], qh[MLA_NOPE + ROPE_HALF:], cos, sin)
        qT_ref[h, 0:MLA_NOPE, :] = qh[0:MLA_NOPE].astype(BF16)
        qT_ref[h, MLA_NOPE:MLA_NOPE + ROPE_HALF, :] = r1.astype(BF16)
        qT_ref[h, MLA_NOPE + ROPE_HALF:MLA_QK, :] = r2.astype(BF16)
        qT_ref[h, MLA_QK:, :] = jnp.zeros((QK_PAD - MLA_QK, R), BF16)

    ckvn = _rms_rows(_dot(w_ref[o_kv:o_kr, :], hn), kvn_ref[...]).astype(BF16)
    kr = _dot(w_ref[o_kr:o_qm, :], hn)
    k1, k2 = _rope_rows(kr[0:ROPE_HALF], kr[ROPE_HALF:], cos, sin)
    krr = jnp.concatenate([k1, k2], axis=0)
    kn = _dot(wk_ref[...], ckvn)
    v = _dot(wv_ref[...], ckvn)
    ones_blk = _ones_row_block(bk)
    for h in range(HEADS):
        _store_keys(k_ref, h, [kn[h * MLA_NOPE:(h + 1) * MLA_NOPE], krr], bk)
        _store_blocks(vT_ref, h, slice(0, HEAD_V), v[h * HEAD_V:(h + 1) * HEAD_V], bk)
        _store_blocks(vT_ref, h, slice(HEAD_V, V_ROWS), ones_blk, bk)

    qmT_ref[...] = (_dot(w_ref[o_qm:o_g, :], hn) * (MEM_DIM ** -0.5 * LOG2E)).astype(BF16)
    gate = _dot(w_ref[o_g:, :], hn)
    gT_ref[...] = (gate * jax.nn.sigmoid(gate)).astype(BF16)


def _proj_fox_kernel(h_ref, g_ref, w_ref, bf_ref, tri_ref,
                     qT_ref, k_ref, vT_ref, qmT_ref, gT_ref, carry_ref, *, bk):
    R = h_ref.shape[-1]

    @pl.when(pl.program_id(1) == 0)
    def _():
        carry_ref[...] = jnp.zeros_like(carry_ref)

    hn = _rms_rows(h_ref[...], g_ref[...]).astype(BF16)
    o_k = MAIN_WIDTH
    o_v = 2 * MAIN_WIDTH
    o_qm = 3 * MAIN_WIDTH
    o_g = o_qm + MEM_WIDTH
    o_f = o_g + D_MODEL

    z = _dot(w_ref[o_f:, :], hn) + bf_ref[...]
    logf = (jnp.minimum(z, 0.0) - jnp.log1p(jnp.exp(-jnp.abs(z)))) * LOG2E
    hi, mid, lo = _split3(logf)
    parts = jnp.concatenate([hi, mid, lo], axis=0).astype(BF16)
    c = _dot(parts, tri_ref[...])
    F = carry_ref[:, 0:1] + (c[0:F_ROWS] + c[F_ROWS:2 * F_ROWS] + c[2 * F_ROWS:])
    carry_ref[...] = jnp.broadcast_to(carry_ref[:, 0:1] + jnp.sum(logf, axis=1, keepdims=True),
                                      carry_ref.shape)
    f_hi, f_mid, f_lo = _split3(F)

    q = (_dot(w_ref[0:o_k, :], hn) * (FOX_DIM ** -0.5 * LOG2E)).astype(BF16)
    k = _dot(w_ref[o_k:o_v, :], hn)
    v = _dot(w_ref[o_v:o_qm, :], hn)
    row = lax.broadcasted_iota(jnp.int32, (AUG_ROWS, R), 0)
    ones_blk = _ones_row_block(bk)
    for h in range(HEADS):
        a, b, c3 = f_hi[h:h + 1], f_mid[h:h + 1], f_lo[h:h + 1]
        q_aug = jnp.where(row < 3, 1.0,
                          jnp.where(row == 3, a, jnp.where(row == 4, b, jnp.where(row == 5, c3, 0.0))))
        k_aug = jnp.where(row == 0, -a,
                          jnp.where(row == 1, -b, jnp.where(row == 2, -c3, jnp.where(row < 6, 1.0, 0.0))))
        qT_ref[h, 0:FOX_DIM, :] = q[h * FOX_DIM:(h + 1) * FOX_DIM]
        qT_ref[h, FOX_DIM:FOX_QK_ROWS, :] = q_aug.astype(BF16)
        qT_ref[h, FOX_QK_ROWS:, :] = jnp.zeros((QK_PAD - FOX_QK_ROWS, R), BF16)
        _store_keys(k_ref, h, [k[h * FOX_DIM:(h + 1) * FOX_DIM], k_aug], bk)
        _store_blocks(vT_ref, h, slice(0, HEAD_V), v[h * HEAD_V:(h + 1) * HEAD_V], bk)
        _store_blocks(vT_ref, h, slice(HEAD_V, V_ROWS), ones_blk, bk)

    qmT_ref[...] = (_dot(w_ref[o_qm:o_g, :], hn) * (MEM_DIM ** -0.5 * LOG2E)).astype(BF16)
    gate = _dot(w_ref[o_g:o_f, :], hn)
    gT_ref[...] = (gate * jax.nn.sigmoid(gate)).astype(BF16)


def _proj_out_specs(B, S, R, bk):
    nb = R // bk
    specs = [
        pl.BlockSpec((None, HEADS, QK_PAD, R), lambda b, s: (b, 0, 0, s)),
        pl.BlockSpec((None, HEADS, nb, bk, QK_PAD), lambda b, s: (b, 0, s, 0, 0)),
        pl.BlockSpec((None, HEADS, nb, V_ROWS, bk), lambda b, s: (b, 0, s, 0, 0)),
        pl.BlockSpec((None, MEM_WIDTH, R), lambda b, s: (b, 0, s)),
        pl.BlockSpec((None, D_MODEL, R), lambda b, s: (b, 0, s)),
    ]
    shapes = [
        jax.ShapeDtypeStruct((B, HEADS, QK_PAD, S), BF16),
        jax.ShapeDtypeStruct((B, HEADS, S // bk, bk, QK_PAD), BF16),
        jax.ShapeDtypeStruct((B, HEADS, S // bk, V_ROWS, bk), BF16),
        jax.ShapeDtypeStruct((B, MEM_WIDTH, S), BF16),
        jax.ShapeDtypeStruct((B, D_MODEL, S), BF16),
    ]
    return specs, shapes


def _proj_mla(h, pos3, inv, g_pre, w_in, q_norm, kv_norm, w_q_up, w_kv_up, *, R, bk, token_major_in=False):
    if token_major_in:
        B, S, D = h.shape
        h_spec = pl.BlockSpec((None, R, D), lambda b, s: (b, s, 0))
    else:
        B, D, S = h.shape
        h_spec = pl.BlockSpec((None, D, R), lambda b, s: (b, 0, s))
    w_inT = w_in.T.astype(BF16)
    wqT = w_q_up.T.astype(BF16)
    wkv = w_kv_up.reshape(KV_LORA, HEADS, MLA_NOPE + HEAD_V)
    wkT = wkv[:, :, :MLA_NOPE].reshape(KV_LORA, HEADS * MLA_NOPE).T.astype(BF16)
    wvT = wkv[:, :, MLA_NOPE:].reshape(KV_LORA, HEADS * HEAD_V).T.astype(BF16)
    out_specs, out_shape = _proj_out_specs(B, S, R, bk)
    if token_major_in:
        out_specs = out_specs + [pl.BlockSpec((None, D, R), lambda b, s: (b, 0, s))]
        out_shape = out_shape + [jax.ShapeDtypeStruct((B, D, S), F32)]
    return pl.pallas_call(
        functools.partial(_proj_mla_kernel, bk=bk),
        grid=(B, S // R),
        in_specs=[
            h_spec,
            pl.BlockSpec((None, 1, R), lambda b, s: (b, 0, s)),
            _const_spec((ROPE_HALF, 1)),
            _const_spec((D, 1)),
            _const_spec(w_inT.shape),
            _const_spec((Q_LORA, 1)),
            _const_spec((KV_LORA, 1)),
            _const_spec(wqT.shape),
            _const_spec(wkT.shape),
            _const_spec(wvT.shape),
        ],
        out_specs=out_specs,
        out_shape=out_shape,
        compiler_params=_params(("arbitrary", "arbitrary")),
        name="proj_mla",
    )(h, pos3, inv, g_pre.reshape(D, 1), w_inT, q_norm.reshape(Q_LORA, 1),
      kv_norm.reshape(KV_LORA, 1), wqT, wkT, wvT)


def _proj_fox(hT, g_pre, w_in, b_f, tri, *, R, bk):
    B, D, S = hT.shape
    n_f = w_in.shape[1] - (3 * MAIN_WIDTH + MEM_WIDTH + D_MODEL)
    o_f = 3 * MAIN_WIDTH
    w_main = jnp.concatenate([w_in[:, :o_f], w_in[:, o_f + n_f:]], axis=1)
    w_f = jnp.pad(w_in[:, o_f:o_f + n_f], ((0, 0), (0, F_ROWS - n_f)))
    w_inT = jnp.concatenate([w_main, w_f], axis=1).T.astype(BF16)
    bf = jnp.pad(b_f, (0, F_ROWS - n_f)).reshape(F_ROWS, 1)
    out_specs, out_shape = _proj_out_specs(B, S, R, bk)
    return pl.pallas_call(
        functools.partial(_proj_fox_kernel, bk=bk),
        grid=(B, S // R),
        in_specs=[
            pl.BlockSpec((None, D, R), lambda b, s: (b, 0, s)),
            _const_spec((D, 1)),
            _const_spec(w_inT.shape),
            _const_spec((F_ROWS, 1)),
            _const_spec((R, R)),
        ],
        out_specs=out_specs,
        out_shape=out_shape,
        scratch_shapes=[pltpu.VMEM((F_ROWS, 128), F32)],
        compiler_params=_params(("arbitrary", "arbitrary")),
        name="proj_fox",
    )(hT, g_pre.reshape(D, 1), w_inT, bf, tri)


def _attn_kernel(qT_ref, k_ref, vT_ref, bias_ref, o_ref, s_a, s_b, p_a, p_b, tm_a, tm_b, al_a, al_b, acc_scr, m_scr):
    qi = pl.program_id(2)
    G, _, bq = qT_ref.shape
    bk = s_a.shape[1]
    r = bq // bk
    s_bufs, p_bufs, tm_bufs, al_bufs = (s_a, s_b), (p_a, p_b), (tm_a, tm_b), (al_a, al_b)
    qTs = [qT_ref[g] for g in range(G)]

    def stat(row):
        return jnp.broadcast_to(row, (STAT_ROWS, bq))

    def qk(g, kb, nxt):
        sT = _dot(k_ref[g, kb], qTs[g])
        s_bufs[nxt][g] = sT
        tm_bufs[nxt][g] = stat(jnp.max(sT, axis=0, keepdims=True))

    def pv(g, kb, prev):
        acc_scr[g] = al_bufs[prev][g][0:1] * acc_scr[g] + _dot(vT_ref[g, kb], p_bufs[prev][g])

    def softmax(g, cur, diag):
        sT = s_bufs[cur][g]
        if diag is None:
            tm = tm_bufs[cur][g]
        else:
            sT = sT + bias_ref[diag]
            tm = stat(jnp.max(sT, axis=0, keepdims=True))
        m_old = m_scr[g]
        m_new = jnp.maximum(m_old, tm)
        p_bufs[cur][g] = jnp.exp2(sT - m_new[0:1]).astype(BF16)
        al_bufs[cur][g] = jnp.exp2(m_old - m_new)
        m_scr[g] = m_new

    def stage(j, cur, diag=None, last=False):
        for g in range(G):
            softmax(g, cur, diag)
        if not last:
            for g in range(G):
                qk(g, j + 1, 1 - cur)
        for g in range(G):
            pv(g, jnp.maximum(j - 1, 0), 1 - cur)

    p_b[...] = jnp.zeros_like(p_b)
    acc_scr[...] = jnp.zeros_like(acc_scr)
    al_b[...] = jnp.ones_like(al_b)
    m_scr[...] = jnp.full_like(m_scr, NEG_BIG)
    for g in range(G):
        qk(g, 0, 0)

    def group(jj, c):
        for t in range(r):
            stage(jj * r + t, t % 2)
        return c

    lax.fori_loop(0, qi, group, 0)
    for t in range(r):
        stage(qi * r + t, t % 2, diag=t, last=(t == r - 1))
    for g in range(G):
        pv(g, qi * r + r - 1, (r - 1) % 2)
        acc = acc_scr[g]
        o_ref[g * HEAD_V:(g + 1) * HEAD_V, :] = (acc[0:HEAD_V] / acc[HEAD_V:HEAD_V + 1]).astype(BF16)


def _attention(qT, k, vT, *, bq, G):
    B, H, dk, S = qT.shape
    nk, bk = k.shape[2], k.shape[3]
    r = bq // bk
    assert bq % (2 * bk) == 0, "a query block must span an even number of key tiles"
    krow = jnp.arange(bk, dtype=jnp.int32)[None, :, None] + bk * jnp.arange(r, dtype=jnp.int32)[:, None, None]
    bias = jnp.where(krow <= jnp.arange(bq, dtype=jnp.int32)[None, None, :], 0.0, NEG_BIG).astype(F32)
    return pl.pallas_call(
        _attn_kernel,
        grid=(B, H // G, S // bq),
        in_specs=[
            pl.BlockSpec((None, G, dk, bq), lambda b, h, i: (b, h, 0, i)),
            pl.BlockSpec((None, G, nk, bk, dk), lambda b, h, i: (b, h, 0, 0, 0)),
            pl.BlockSpec((None, G, nk, V_ROWS, bk), lambda b, h, i: (b, h, 0, 0, 0)),
            _const_spec((r, bk, bq)),
        ],
        out_specs=pl.BlockSpec((None, G * HEAD_V, bq), lambda b, h, i: (b, h, i)),
        out_shape=jax.ShapeDtypeStruct((B, H * HEAD_V, S), BF16),
        scratch_shapes=[pltpu.VMEM((G, bk, bq), F32), pltpu.VMEM((G, bk, bq), F32),
                        pltpu.VMEM((G, bk, bq), BF16), pltpu.VMEM((G, bk, bq), BF16),
                        pltpu.VMEM((G, STAT_ROWS, bq), F32), pltpu.VMEM((G, STAT_ROWS, bq), F32),
                        pltpu.VMEM((G, STAT_ROWS, bq), F32), pltpu.VMEM((G, STAT_ROWS, bq), F32),
                        pltpu.VMEM((G, V_ROWS, bq), F32), pltpu.VMEM((G, STAT_ROWS, bq), F32)],
        compiler_params=_params(("arbitrary", "arbitrary", "arbitrary")),
        name="causal_attn",
    )(qT, k, vT, bias)


def _out_kernel(y_ref, qm_ref, g_ref, h_ref, km_ref, vmT_ref, w_ref, gpost_ref, o_ref, y_scr, *, token_major_out):
    y_scr[0:MAIN_WIDTH, :] = (y_ref[...].astype(F32) * g_ref[0:MAIN_WIDTH, :].astype(F32)).astype(BF16)
    for hm in range(MEM_HEADS):
        rows = slice(hm * MEM_DIM, (hm + 1) * MEM_DIM)
        sT = _dot(km_ref[hm], qm_ref[rows, :])
        p = jnp.exp2(sT - jnp.max(sT, axis=0, keepdims=True))
        l = jnp.sum(p, axis=0, keepdims=True)
        o = _dot(vmT_ref[hm], p.astype(BF16)) / l
        grow = slice(MAIN_WIDTH + hm * MEM_DIM, MAIN_WIDTH + (hm + 1) * MEM_DIM)
        y_scr[grow, :] = (o * g_ref[grow, :].astype(F32)).astype(BF16)
    out = _dot(w_ref[...], y_scr[...])
    h_new = h_ref[...] + _rms_rows(out, gpost_ref[...])
    o_ref[...] = h_new.T if token_major_out else h_new


def _out_layer(yT, qmT, gT, hT, km, vmT, layer, w_out, g_post, *, R, token_major_out=False):
    B, D, S = hT.shape
    M = km.shape[3]
    w_outT = w_out.T.astype(BF16)
    if token_major_out:
        out_spec = pl.BlockSpec((None, R, D), lambda b, s: (b, s, 0))
        out_shape, aliases = jax.ShapeDtypeStruct((B, S, D), F32), {}
    else:
        out_spec = pl.BlockSpec((None, D, R), lambda b, s: (b, 0, s))
        out_shape, aliases = jax.ShapeDtypeStruct((B, D, S), F32), {3: 0}
    return pl.pallas_call(
        functools.partial(_out_kernel, token_major_out=token_major_out),
        grid=(B, S // R),
        in_specs=[
            pl.BlockSpec((None, MAIN_WIDTH, R), lambda b, s: (b, 0, s)),
            pl.BlockSpec((None, MEM_WIDTH, R), lambda b, s: (b, 0, s)),
            pl.BlockSpec((None, D, R), lambda b, s: (b, 0, s)),
            pl.BlockSpec((None, D, R), lambda b, s: (b, 0, s)),
            pl.BlockSpec((None, None, MEM_HEADS, M, MEM_DIM), lambda b, s: (b, layer, 0, 0, 0)),
            pl.BlockSpec((None, None, MEM_HEADS, MEM_DIM, M), lambda b, s: (b, layer, 0, 0, 0)),
            _const_spec((D, D)),
            _const_spec((D, 1)),
        ],
        out_specs=out_spec,
        out_shape=out_shape,
        scratch_shapes=[pltpu.VMEM((D, R), BF16)],
        input_output_aliases=aliases,
        compiler_params=_params(("arbitrary", "arbitrary")),
        name="out_layer",
    )(yT, qmT, gT, hT, km, vmT, w_outT, g_post.reshape(D, 1))


def _forward(x, mem, positions, norm_pre, norm_post, mem_norm, w_mem_kv, w_out,
             w_in_a, q_norm_a, kv_norm_a, w_q_up_a, w_kv_up_a, w_in_b, b_f, *, R, bq, bk, G):
    B, S, D = x.shape
    depth = norm_pre.shape[0]
    km, vmT = _mem_kv(mem, mem_norm, w_mem_kv)
    pos3 = positions.reshape(B, 1, S)
    inv = (ROPE_THETA ** (-jnp.arange(ROPE_HALF, dtype=F32) / ROPE_HALF)).reshape(ROPE_HALF, 1)
    tri = jnp.triu(jnp.ones((R, R), F32)).astype(BF16)
    h = x
    for i in range(depth):
        j = i // 2
        if i % 2 == 0:
            outs = _proj_mla(h, pos3, inv, norm_pre[i], w_in_a[j], q_norm_a[j], kv_norm_a[j],
                             w_q_up_a[j], w_kv_up_a[j], R=R, bk=bk, token_major_in=(i == 0))
            if i == 0:
                h = outs[5]
            qT, kT, vT, qmT, gT = outs[:5]
        else:
            qT, kT, vT, qmT, gT = _proj_fox(h, norm_pre[i], w_in_b[j], b_f[j], tri, R=R, bk=bk)
        yT = _attention(qT, kT, vT, bq=bq, G=G)
        h = _out_layer(yT, qmT, gT, h, km, vmT, i, w_out[i], norm_post[i], R=R,
                       token_major_out=(i == depth - 1))
    return h


def kernel(x, mem, positions, norm_pre, norm_post, mem_norm, w_mem_kv, w_out, w_in_a, q_norm_a,
           kv_norm_a, w_q_up_a, w_kv_up_a, w_in_b, b_f):
    return _forward(x, mem, positions, norm_pre, norm_post, mem_norm, w_mem_kv, w_out, w_in_a,
                    q_norm_a, kv_norm_a, w_q_up_a, w_kv_up_a, w_in_b, b_f, R=512, bq=512, bk=256, G=4)
```

```python
import functools
import math

import jax
import jax.numpy as jnp
from jax import lax
from jax.experimental import pallas as pl
from jax.experimental.pallas import tpu as pltpu

D_MODEL = 1024
N_MEM = 256
MEM_WIDTH = D_MODEL // 4
MAIN_WIDTH = D_MODEL - MEM_WIDTH
MEM_HEADS = 4
MEM_DIM = MEM_WIDTH // MEM_HEADS
HEADS = 12
HEAD_V = 64
MLA_NOPE = 64
MLA_ROPE = 32
MLA_QK = MLA_NOPE + MLA_ROPE
ROPE_HALF = MLA_ROPE // 2
Q_LORA = 384
KV_LORA = 256
ROPE_THETA = 10000.0
FOX_DIM = 64
EPS = 1e-6
LOG2E = math.log2(math.e)

BF16_ROWS = 16
AUG_ROWS = BF16_ROWS
FOX_QK_ROWS = FOX_DIM + AUG_ROWS
V_ROWS = HEAD_V + BF16_ROWS
QK_PAD = 128
STAT_ROWS = 1
F_ROWS = BF16_ROWS
NEG_BIG = -1e30
VMEM_LIMIT_BYTES = 56 * 1024 * 1024

F32 = jnp.float32
BF16 = jnp.bfloat16


def _dot(a, b):
    return jnp.dot(a, b, preferred_element_type=F32)


def _rms_rows(x, g):
    ms = jnp.mean(x * x, axis=0, keepdims=True)
    return x * lax.rsqrt(ms + EPS) * g


def _split3(x):
    hi = x.astype(BF16).astype(F32)
    r = x - hi
    mid = r.astype(BF16).astype(F32)
    lo = (r - mid).astype(BF16).astype(F32)
    return hi, mid, lo


def _rope_rows(x1, x2, cos, sin):
    return x1 * cos - x2 * sin, x2 * cos + x1 * sin


def _const_spec(shape):
    return pl.BlockSpec(shape, lambda *_: (0,) * len(shape))


def _params(semantics):
    return pltpu.CompilerParams(dimension_semantics=semantics,
                                vmem_limit_bytes=VMEM_LIMIT_BYTES)


def _mem_kv_kernel(mem_ref, g_ref, wk_ref, wvT_ref, km_ref, vmT_ref, *, depth):
    x = mem_ref[...]
    ms = jnp.mean(x * x, axis=-1, keepdims=True)
    xn = (x * lax.rsqrt(ms + EPS) * g_ref[...]).astype(BF16)
    for i in range(depth):
        for h in range(MEM_HEADS):
            km_ref[i, h] = _dot(xn, wk_ref[i, h]).astype(BF16)
            vmT_ref[i, h] = lax.dot_general(
                wvT_ref[i, h], xn, (((1,), (1,)), ((), ())),
                preferred_element_type=F32).astype(BF16)


def _mem_kv(mem, mem_norm, w_mem_kv):
    B, M, D = mem.shape
    depth = w_mem_kv.shape[0]
    w = w_mem_kv.reshape(depth, D, 2, MEM_HEADS, MEM_DIM)
    wk = jnp.transpose(w[:, :, 0], (0, 2, 1, 3)).astype(BF16)
    wvT = jnp.transpose(w[:, :, 1], (0, 2, 3, 1)).astype(BF16)
    return pl.pallas_call(
        functools.partial(_mem_kv_kernel, depth=depth),
        grid=(B,),
        in_specs=[
            pl.BlockSpec((None, M, D), lambda b: (b, 0, 0)),
            _const_spec((1, D)),
            _const_spec((depth, MEM_HEADS, D, MEM_DIM)),
            _const_spec((depth, MEM_HEADS, MEM_DIM, D)),
        ],
        out_specs=[
            pl.BlockSpec((None, depth, MEM_HEADS, M, MEM_DIM), lambda b: (b, 0, 0, 0, 0)),
            pl.BlockSpec((None, depth, MEM_HEADS, MEM_DIM, M), lambda b: (b, 0, 0, 0, 0)),
        ],
        out_shape=[
            jax.ShapeDtypeStruct((B, depth, MEM_HEADS, M, MEM_DIM), BF16),
            jax.ShapeDtypeStruct((B, depth, MEM_HEADS, MEM_DIM, M), BF16),
        ],
        compiler_params=_params(("arbitrary",)),
        name="mem_kv",
    )(mem, mem_norm.reshape(1, D), wk, wvT)


def _ones_row_block(cols):
    row = lax.broadcasted_iota(jnp.int32, (BF16_ROWS, cols), 0)
    return jnp.where(row == 0, 1.0, 0.0)


def _store_blocks(ref, h, rows, val, bk):
    nblk = ref.shape[1]
    same = val.shape[1] == bk
    for j in range(nblk):
        ref[h, j, rows, :] = (val if same else val[:, j * bk:(j + 1) * bk]).astype(BF16)


def _store_keys(k_ref, h, parts, bk):
    R = parts[0].shape[1]
    rows = sum(p.shape[0] for p in parts)
    kT = jnp.concatenate(list(parts) + [jnp.zeros((QK_PAD - rows, R), F32)], axis=0)
    for j in range(k_ref.shape[1]):
        k_ref[h, j] = kT[:, j * bk:(j + 1) * bk].T.astype(BF16)


def _proj_mla_kernel(h_ref, pos_ref, inv_ref, g_ref, w_ref, qn_ref, kvn_ref, wq_ref, wk_ref, wv_ref,
                     qT_ref, k_ref, vT_ref, qmT_ref, gT_ref, *hT_out, bk):
    if hT_out:
        hT = h_ref[...].T
        hT_out[0][...] = hT
    else:
        hT = h_ref[...]
    R = hT.shape[-1]
    hn = _rms_rows(hT, g_ref[...]).astype(BF16)
    o_kv = Q_LORA
    o_kr = o_kv + KV_LORA
    o_qm = o_kr + MLA_ROPE
    o_g = o_qm + MEM_WIDTH

    ang = inv_ref[...] * pos_ref[...].astype(F32)
    cos, sin = jnp.cos(ang), jnp.sin(ang)

    cqn = _rms_rows(_dot(w_ref[0:o_kv, :], hn), qn_ref[...]).astype(BF16)
    q = _dot(wq_ref[...], cqn) * (MLA_QK ** -0.5 * LOG2E)
    for h in range(HEADS):
        qh = q[h * MLA_QK:(h + 1) * MLA_QK]
        r1, r2 = _rope_rows(qh[MLA_NOPE:MLA_NOPE + ROPE_HALF], qh[MLA_NOPE + ROPE_HALF:], cos, sin)
        qT_ref[h, 0:MLA_NOPE, :] = qh[0:MLA_NOPE].astype(BF16)
        qT_ref[h, MLA_NOPE:MLA_NOPE + ROPE_HALF, :] = r1.astype(BF16)
        qT_ref[h, MLA_NOPE + ROPE_HALF:MLA_QK, :] = r2.astype(BF16)
        qT_ref[h, MLA_QK:, :] = jnp.zeros((QK_PAD - MLA_QK, R), BF16)

    ckvn = _rms_rows(_dot(w_ref[o_kv:o_kr, :], hn), kvn_ref[...]).astype(BF16)
    kr = _dot(w_ref[o_kr:o_qm, :], hn)
    k1, k2 = _rope_rows(kr[0:ROPE_HALF], kr[ROPE_HALF:], cos, sin)
    krr = jnp.concatenate([k1, k2], axis=0)
    kn = _dot(wk_ref[...], ckvn)
    v = _dot(wv_ref[...], ckvn)
    ones_blk = _ones_row_block(bk)
    for h in range(HEADS):
        _store_keys(k_ref, h, [kn[h * MLA_NOPE:(h + 1) * MLA_NOPE], krr], bk)
        _store_blocks(vT_ref, h, slice(0, HEAD_V), v[h * HEAD_V:(h + 1) * HEAD_V], bk)
        _store_blocks(vT_ref, h, slice(HEAD_V, V_ROWS), ones_blk, bk)

    qmT_ref[...] = (_dot(w_ref[o_qm:o_g, :], hn) * (MEM_DIM ** -0.5 * LOG2E)).astype(BF16)
    gate = _dot(w_ref[o_g:, :], hn)
    gT_ref[...] = (gate * jax.nn.sigmoid(gate)).astype(BF16)


def _proj_fox_kernel(h_ref, g_ref, w_ref, bf_ref, tri_ref,
                     qT_ref, k_ref, vT_ref, qmT_ref, gT_ref, carry_ref, *, bk):
    R = h_ref.shape[-1]

    @pl.when(pl.program_id(1) == 0)
    def _():
        carry_ref[...] = jnp.zeros_like(carry_ref)

    hn = _rms_rows(h_ref[...], g_ref[...]).astype(BF16)
    o_k = MAIN_WIDTH
    o_v = 2 * MAIN_WIDTH
    o_qm = 3 * MAIN_WIDTH
    o_g = o_qm + MEM_WIDTH
    o_f = o_g + D_MODEL

    z = _dot(w_ref[o_f:, :], hn) + bf_ref[...]
    logf = (jnp.minimum(z, 0.0) - jnp.log1p(jnp.exp(-jnp.abs(z)))) * LOG2E
    hi, mid, lo = _split3(logf)
    parts = jnp.concatenate([hi, mid, lo], axis=0).astype(BF16)
    c = _dot(parts, tri_ref[...])
    F = carry_ref[:, 0:1] + (c[0:F_ROWS] + c[F_ROWS:2 * F_ROWS] + c[2 * F_ROWS:])
    carry_ref[...] = jnp.broadcast_to(carry_ref[:, 0:1] + jnp.sum(logf, axis=1, keepdims=True),
                                      carry_ref.shape)
    f_hi, f_mid, f_lo = _split3(F)

    q = (_dot(w_ref[0:o_k, :], hn) * (FOX_DIM ** -0.5 * LOG2E)).astype(BF16)
    k = _dot(w_ref[o_k:o_v, :], hn)
    v = _dot(w_ref[o_v:o_qm, :], hn)
    row = lax.broadcasted_iota(jnp.int32, (AUG_ROWS, R), 0)
    ones_blk = _ones_row_block(bk)
    for h in range(HEADS):
        a, b, c3 = f_hi[h:h + 1], f_mid[h:h + 1], f_lo[h:h + 1]
        q_aug = jnp.where(row < 3, 1.0,
                          jnp.where(row == 3, a, jnp.where(row == 4, b, jnp.where(row == 5, c3, 0.0))))
        k_aug = jnp.where(row == 0, -a,
                          jnp.where(row == 1, -b, jnp.where(row == 2, -c3, jnp.where(row < 6, 1.0, 0.0))))
        qT_ref[h, 0:FOX_DIM, :] = q[h * FOX_DIM:(h + 1) * FOX_DIM]
        qT_ref[h, FOX_DIM:FOX_QK_ROWS, :] = q_aug.astype(BF16)
        qT_ref[h, FOX_QK_ROWS:, :] = jnp.zeros((QK_PAD - FOX_QK_ROWS, R), BF16)
        _store_keys(k_ref, h, [k[h * FOX_DIM:(h + 1) * FOX_DIM], k_aug], bk)
        _store_blocks(vT_ref, h, slice(0, HEAD_V), v[h * HEAD_V:(h + 1) * HEAD_V], bk)
        _store_blocks(vT_ref, h, slice(HEAD_V, V_ROWS), ones_blk, bk)

    qmT_ref[...] = (_dot(w_ref[o_qm:o_g, :], hn) * (MEM_DIM ** -0.5 * LOG2E)).astype(BF16)
    gate = _dot(w_ref[o_g:o_f, :], hn)
    gT_ref[...] = (gate * jax.nn.sigmoid(gate)).astype(BF16)


def _proj_out_specs(B, S, R, bk):
    nb = R // bk
    specs = [
        pl.BlockSpec((None, HEADS, QK_PAD, R), lambda b, s: (b, 0, 0, s)),
        pl.BlockSpec((None, HEADS, nb, bk, QK_PAD), lambda b, s: (b, 0, s, 0, 0)),
        pl.BlockSpec((None, HEADS, nb, V_ROWS, bk), lambda b, s: (b, 0, s, 0, 0)),
        pl.BlockSpec((None, MEM_WIDTH, R), lambda b, s: (b, 0, s)),
        pl.BlockSpec((None, D_MODEL, R), lambda b, s: (b, 0, s)),
    ]
    shapes = [
        jax.ShapeDtypeStruct((B, HEADS, QK_PAD, S), BF16),
        jax.ShapeDtypeStruct((B, HEADS, S // bk, bk, QK_PAD), BF16),
        jax.ShapeDtypeStruct((B, HEADS, S // bk, V_ROWS, bk), BF16),
        jax.ShapeDtypeStruct((B, MEM_WIDTH, S), BF16),
        jax.ShapeDtypeStruct((B, D_MODEL, S), BF16),
    ]
    return specs, shapes


def _proj_mla(h, pos3, inv, g_pre, w_in, q_norm, kv_norm, w_q_up, w_kv_up, *, R, bk, token_major_in=False):
    if token_major_in:
        B, S, D = h.shape
        h_spec = pl.BlockSpec((None, R, D), lambda b, s: (b, s, 0))
    else:
        B, D, S = h.shape
        h_spec = pl.BlockSpec((None, D, R), lambda b, s: (b, 0, s))
    w_inT = w_in.T.astype(BF16)
    wqT = w_q_up.T.astype(BF16)
    wkv = w_kv_up.reshape(KV_LORA, HEADS, MLA_NOPE + HEAD_V)
    wkT = wkv[:, :, :MLA_NOPE].reshape(KV_LORA, HEADS * MLA_NOPE).T.astype(BF16)
    wvT = wkv[:, :, MLA_NOPE:].reshape(KV_LORA, HEADS * HEAD_V).T.astype(BF16)
    out_specs, out_shape = _proj_out_specs(B, S, R, bk)
    if token_major_in:
        out_specs = out_specs + [pl.BlockSpec((None, D, R), lambda b, s: (b, 0, s))]
        out_shape = out_shape + [jax.ShapeDtypeStruct((B, D, S), F32)]
    return pl.pallas_call(
        functools.partial(_proj_mla_kernel, bk=bk),
        grid=(B, S // R),
        in_specs=[
            h_spec,
            pl.BlockSpec((None, 1, R), lambda b, s: (b, 0, s)),
            _const_spec((ROPE_HALF, 1)),
            _const_spec((D, 1)),
            _const_spec(w_inT.shape),
            _const_spec((Q_LORA, 1)),
            _const_spec((KV_LORA, 1)),
            _const_spec(wqT.shape),
            _const_spec(wkT.shape),
            _const_spec(wvT.shape),
        ],
        out_specs=out_specs,
        out_shape=out_shape,
        compiler_params=_params(("arbitrary", "arbitrary")),
        name="proj_mla",
    )(h, pos3, inv, g_pre.reshape(D, 1), w_inT, q_norm.reshape(Q_LORA, 1),
      kv_norm.reshape(KV_LORA, 1), wqT, wkT, wvT)


def _proj_fox(hT, g_pre, w_in, b_f, tri, *, R, bk):
    B, D, S = hT.shape
    n_f = w_in.shape[1] - (3 * MAIN_WIDTH + MEM_WIDTH + D_MODEL)
    o_f = 3 * MAIN_WIDTH
    w_main = jnp.concatenate([w_in[:, :o_f], w_in[:, o_f + n_f:]], axis=1)
    w_f = jnp.pad(w_in[:, o_f:o_f + n_f], ((0, 0), (0, F_ROWS - n_f)))
    w_inT = jnp.concatenate([w_main, w_f], axis=1).T.astype(BF16)
    bf = jnp.pad(b_f, (0, F_ROWS - n_f)).reshape(F_ROWS, 1)
    out_specs, out_shape = _proj_out_specs(B, S, R, bk)
    return pl.pallas_call(
        functools.partial(_proj_fox_kernel, bk=bk),
        grid=(B, S // R),
        in_specs=[
            pl.BlockSpec((None, D, R), lambda b, s: (b, 0, s)),
            _const_spec((D, 1)),
            _const_spec(w_inT.shape),
            _const_spec((F_ROWS, 1)),
            _const_spec((R, R)),
        ],
        out_specs=out_specs,
        out_shape=out_shape,
        scratch_shapes=[pltpu.VMEM((F_ROWS, 128), F32)],
        compiler_params=_params(("arbitrary", "arbitrary")),
        name="proj_fox",
    )(hT, g_pre.reshape(D, 1), w_inT, bf, tri)


def _attn_kernel(qT_ref, k_ref, vT_ref, bias_ref, o_ref, s_a, s_b, p_a, p_b, tm_a, tm_b, al_a, al_b, acc_scr, m_scr):
    qi = pl.program_id(2)
    G, _, bq = qT_ref.shape
    bk = s_a.shape[1]
    r = bq // bk
    s_bufs, p_bufs, tm_bufs, al_bufs = (s_a, s_b), (p_a, p_b), (tm_a, tm_b), (al_a, al_b)
    qTs = [qT_ref[g] for g in range(G)]

    def stat(row):
        return jnp.broadcast_to(row, (STAT_ROWS, bq))

    def qk(g, kb, nxt):
        sT = _dot(k_ref[g, kb], qTs[g])
        s_bufs[nxt][g] = sT
        tm_bufs[nxt][g] = stat(jnp.max(sT, axis=0, keepdims=True))

    def pv(g, kb, prev):
        acc_scr[g] = al_bufs[prev][g][0:1] * acc_scr[g] + _dot(vT_ref[g, kb], p_bufs[prev][g])

    def softmax(g, cur, diag):
        sT = s_bufs[cur][g]
        if diag is None:
            tm = tm_bufs[cur][g]
        else:
            sT = sT + bias_ref[diag]
            tm = stat(jnp.max(sT, axis=0, keepdims=True))
        m_old = m_scr[g]
        m_new = jnp.maximum(m_old, tm)
        p_bufs[cur][g] = jnp.exp2(sT - m_new[0:1]).astype(BF16)
        al_bufs[cur][g] = jnp.exp2(m_old - m_new)
        m_scr[g] = m_new

    def stage(j, cur, diag=None, last=False):
        for g in range(G):
            softmax(g, cur, diag)
        if not last:
            for g in range(G):
                qk(g, j + 1, 1 - cur)
        for g in range(G):
            pv(g, jnp.maximum(j - 1, 0), 1 - cur)

    p_b[...] = jnp.zeros_like(p_b)
    acc_scr[...] = jnp.zeros_like(acc_scr)
    al_b[...] = jnp.ones_like(al_b)
    m_scr[...] = jnp.full_like(m_scr, NEG_BIG)
    for g in range(G):
        qk(g, 0, 0)

    def group(jj, c):
        for t in range(r):
            stage(jj * r + t, t % 2)
        return c

    lax.fori_loop(0, qi, group, 0)
    for t in range(r):
        stage(qi * r + t, t % 2, diag=t, last=(t == r - 1))
    for g in range(G):
        pv(g, qi * r + r - 1, (r - 1) % 2)
        acc = acc_scr[g]
        o_ref[g * HEAD_V:(g + 1) * HEAD_V, :] = (acc[0:HEAD_V] / acc[HEAD_V:HEAD_V + 1]).astype(BF16)


def _attention(qT, k, vT, *, bq, G):
    B, H, dk, S = qT.shape
    nk, bk = k.shape[2], k.shape[3]
    r = bq // bk
    assert bq % (2 * bk) == 0, "a query block must span an even number of key tiles"
    krow = jnp.arange(bk, dtype=jnp.int32)[None, :, None] + bk * jnp.arange(r, dtype=jnp.int32)[:, None, None]
    bias = jnp.where(krow <= jnp.arange(bq, dtype=jnp.int32)[None, None, :], 0.0, NEG_BIG).astype(F32)
    return pl.pallas_call(
        _attn_kernel,
        grid=(B, H // G, S // bq),
        in_specs=[
            pl.BlockSpec((None, G, dk, bq), lambda b, h, i: (b, h, 0, i)),
            pl.BlockSpec((None, G, nk, bk, dk), lambda b, h, i: (b, h, 0, 0, 0)),
            pl.BlockSpec((None, G, nk, V_ROWS, bk), lambda b, h, i: (b, h, 0, 0, 0)),
            _const_spec((r, bk, bq)),
        ],
        out_specs=pl.BlockSpec((None, G * HEAD_V, bq), lambda b, h, i: (b, h, i)),
        out_shape=jax.ShapeDtypeStruct((B, H * HEAD_V, S), BF16),
        scratch_shapes=[pltpu.VMEM((G, bk, bq), F32), pltpu.VMEM((G, bk, bq), F32),
                        pltpu.VMEM((G, bk, bq), BF16), pltpu.VMEM((G, bk, bq), BF16),
                        pltpu.VMEM((G, STAT_ROWS, bq), F32), pltpu.VMEM((G, STAT_ROWS, bq), F32),
                        pltpu.VMEM((G, STAT_ROWS, bq), F32), pltpu.VMEM((G, STAT_ROWS, bq), F32),
                        pltpu.VMEM((G, V_ROWS, bq), F32), pltpu.VMEM((G, STAT_ROWS, bq), F32)],
        compiler_params=_params(("arbitrary", "arbitrary", "arbitrary")),
        name="causal_attn",
    )(qT, k, vT, bias)


def _out_kernel(y_ref, qm_ref, g_ref, h_ref, km_ref, vmT_ref, w_ref, gpost_ref, o_ref, y_scr, *, token_major_out):
    y_scr[0:MAIN_WIDTH, :] = (y_ref[...].astype(F32) * g_ref[0:MAIN_WIDTH, :].astype(F32)).astype(BF16)
    for hm in range(MEM_HEADS):
        rows = slice(hm * MEM_DIM, (hm + 1) * MEM_DIM)
        sT = _dot(km_ref[hm], qm_ref[rows, :])
        p = jnp.exp2(sT - jnp.max(sT, axis=0, keepdims=True))
        l = jnp.sum(p, axis=0, keepdims=True)
        o = _dot(vmT_ref[hm], p.astype(BF16)) / l
        grow = slice(MAIN_WIDTH + hm * MEM_DIM, MAIN_WIDTH + (hm + 1) * MEM_DIM)
        y_scr[grow, :] = (o * g_ref[grow, :].astype(F32)).astype(BF16)
    out = _dot(w_ref[...], y_scr[...])
    h_new = h_ref[...] + _rms_rows(out, gpost_ref[...])
    o_ref[...] = h_new.T if token_major_out else h_new


def _out_layer(yT, qmT, gT, hT, km, vmT, layer, w_out, g_post, *, R, token_major_out=False):
    B, D, S = hT.shape
    M = km.shape[3]
    w_outT = w_out.T.astype(BF16)
    if token_major_out:
        out_spec = pl.BlockSpec((None, R, D), lambda b, s: (b, s, 0))
        out_shape, aliases = jax.ShapeDtypeStruct((B, S, D), F32), {}
    else:
        out_spec = pl.BlockSpec((None, D, R), lambda b, s: (b, 0, s))
        out_shape, aliases = jax.ShapeDtypeStruct((B, D, S), F32), {3: 0}
    return pl.pallas_call(
        functools.partial(_out_kernel, token_major_out=token_major_out),
        grid=(B, S // R),
        in_specs=[
            pl.BlockSpec((None, MAIN_WIDTH, R), lambda b, s: (b, 0, s)),
            pl.BlockSpec((None, MEM_WIDTH, R), lambda b, s: (b, 0, s)),
            pl.BlockSpec((None, D, R), lambda b, s: (b, 0, s)),
            pl.BlockSpec((None, D, R), lambda b, s: (b, 0, s)),
            pl.BlockSpec((None, None, MEM_HEADS, M, MEM_DIM), lambda b, s: (b, layer, 0, 0, 0)),
            pl.BlockSpec((None, None, MEM_HEADS, MEM_DIM, M), lambda b, s: (b, layer, 0, 0, 0)),
            _const_spec((D, D)),
            _const_spec((D, 1)),
        ],
        out_specs=out_spec,
        out_shape=out_shape,
        scratch_shapes=[pltpu.VMEM((D, R), BF16)],
        input_output_aliases=aliases,
        compiler_params=_params(("arbitrary", "arbitrary")),
        name="out_layer",
    )(yT, qmT, gT, hT, km, vmT, w_outT, g_post.reshape(D, 1))


def _forward(x, mem, positions, norm_pre, norm_post, mem_norm, w_mem_kv, w_out,
             w_in_a, q_norm_a, kv_norm_a, w_q_up_a, w_kv_up_a, w_in_b, b_f, *, R, bq, bk, G):
    B, S, D = x.shape
    depth = norm_pre.shape[0]
    km, vmT = _mem_kv(mem, mem_norm, w_mem_kv)
    pos3 = positions.reshape(B, 1, S)
    inv = (ROPE_THETA ** (-jnp.arange(ROPE_HALF, dtype=F32) / ROPE_HALF)).reshape(ROPE_HALF, 1)
    tri = jnp.triu(jnp.ones((R, R), F32)).astype(BF16)
    h = x
    for i in range(depth):
        j = i // 2
        if i % 2 == 0:
            outs = _proj_mla(h, pos3, inv, norm_pre[i], w_in_a[j], q_norm_a[j], kv_norm_a[j],
                             w_q_up_a[j], w_kv_up_a[j], R=R, bk=bk, token_major_in=(i == 0))
            if i == 0:
                h = outs[5]
            qT, kT, vT, qmT, gT = outs[:5]
        else:
            qT, kT, vT, qmT, gT = _proj_fox(h, norm_pre[i], w_in_b[j], b_f[j], tri, R=R, bk=bk)
        yT = _attention(qT, kT, vT, bq=bq, G=G)
        h = _out_layer(yT, qmT, gT, h, km, vmT, i, w_out[i], norm_post[i], R=R,
                       token_major_out=(i == depth - 1))
    return h


def kernel(x, mem, positions, norm_pre, norm_post, mem_norm, w_mem_kv, w_out, w_in_a, q_norm_a,
           kv_norm_a, w_q_up_a, w_kv_up_a, w_in_b, b_f):
    return _forward(x, mem, positions, norm_pre, norm_post, mem_norm, w_mem_kv, w_out, w_in_a,
                    q_norm_a, kv_norm_a, w_q_up_a, w_kv_up_a, w_in_b, b_f, R=512, bq=512, bk=256, G=4)
```
